```python
import jax, jax.numpy as jnp
from jax import lax
import numpy as np

D_MODEL = 2048
BATCH = 4
SEQ = 2048
DEPTH = 2
DEC_BATCH = 128
DEC_SEQ = 4
PAST_LEN = 16384
PAGE_SIZE = 128

D_MIX = D_MODEL
D_CONV = D_MIX // 2
N_CONV_HEADS = 4
D_POOL = D_MIX - D_CONV
N_POOL_GROUPS = 4
POOL_GROUP_W = D_POOL // N_POOL_GROUPS
POOL_WINDOWS = (2, 4, 8, 16)
POOL_MAX = 16
CONV_W = 31
D_FF = 4 * D_MODEL
D_IN = 2 * D_CONV + D_POOL
EPS = 1e-6

kernel_name = "hymba_style_conformer_conv_multiscale_pool_decoder_step"


def rmsnorm(x, g):
    xf = x.astype(jnp.float32)
    y = xf * lax.rsqrt(jnp.mean(xf * xf, axis=-1, keepdims=True) + EPS)
    return (y * g.astype(jnp.float32)).astype(x.dtype)


def layernorm(x, g, b):
    xf = x.astype(jnp.float32)
    mu = jnp.mean(xf, axis=-1, keepdims=True)
    var = jnp.mean(jnp.square(xf - mu), axis=-1, keepdims=True)
    y = (xf - mu) * lax.rsqrt(var + EPS)
    return (y * g.astype(jnp.float32) + b.astype(jnp.float32)).astype(x.dtype)


def depthwise_causal_conv(u_ext, w, b):
    c = u_ext.shape[-1]
    y = lax.conv_general_dilated(
        u_ext, w[:, None, :].astype(u_ext.dtype), window_strides=(1,), padding="VALID",
        dimension_numbers=("NWC", "WIO", "NWC"), feature_group_count=c)
    return y + b


def multiscale_pool(z_ext, pos0):
    bsz, L, _ = z_ext.shape
    H = POOL_MAX - 1
    T = L - H
    zf = z_ext.astype(jnp.float32).reshape(bsz, L, N_POOL_GROUPS, POOL_GROUP_W)
    cs = jnp.concatenate([jnp.zeros_like(zf[:, :1]), jnp.cumsum(zf, axis=1)], axis=1)
    win = jnp.array(POOL_WINDOWS, dtype=jnp.int32)
    hi = H + 1 + jnp.arange(T, dtype=jnp.int32)
    lo = hi[:, None] - win[None, :]
    g_idx = jnp.arange(N_POOL_GROUPS)[None, :]
    s = cs[:, hi] - cs[:, lo, g_idx]
    pos = pos0 + jnp.arange(T, dtype=jnp.int32)
    cnt = jnp.minimum(win[None, :], pos[:, None] + 1).astype(jnp.float32)
    d = s / cnt[None, :, :, None] - zf[:, H:]
    return d.astype(z_ext.dtype)


def trunk_layer(x, hist_conv, hist_pool, pos0, norm1_g, w_in, b_in, conv_w, conv_b,
                ln_g, ln_b, pool_w, pool_scale, w_out, norm2_g, w_ff1, w_ff2):
    bsz, T, _ = x.shape
    h = rmsnorm(x, norm1_g)
    proj = jnp.einsum("btd,de->bte", h, w_in) + b_in
    a, gate, z = proj[..., :D_CONV], proj[..., D_CONV:2 * D_CONV], proj[..., 2 * D_CONV:]
    u = a * jax.nn.sigmoid(gate)
    u_ext = jnp.concatenate([hist_conv.astype(u.dtype), u], axis=1)
    c = jax.nn.silu(layernorm(depthwise_causal_conv(u_ext, conv_w, conv_b), ln_g, ln_b))
    z_ext = jnp.concatenate([hist_pool.astype(z.dtype), z], axis=1)
    d = multiscale_pool(z_ext, pos0)
    p = jnp.einsum("btgc,gce->btge", d, pool_w).reshape(bsz, T, D_POOL) * pool_scale
    mix = jnp.concatenate([c, p], axis=-1)
    x = x + jnp.einsum("btm,md->btd", mix, w_out)
    h2 = rmsnorm(x, norm2_g)
    f = jnp.square(jax.nn.relu(jnp.einsum("btd,df->btf", h2, w_ff1)))
    x = x + jnp.einsum("btf,fd->btd", f, w_ff2)
    return x, u_ext[:, -(CONV_W - 1):], z_ext[:, -(POOL_MAX - 1):]


def setup_inputs(seed: int = 0) -> dict:
    key = jax.random.key(seed)
    ks = jax.random.split(key, 20)
    n = jax.random.normal
    f32 = jnp.float32
    return {
        "x_prompt": n(ks[0], (BATCH, SEQ, D_MODEL), f32),
        "x_sample": n(ks[1], (DEC_BATCH, DEC_SEQ, D_MODEL), f32),
        "state_conv": 0.5 * n(ks[2], (DEPTH, DEC_BATCH, CONV_W - 1, D_CONV), f32),
        "state_pool": n(ks[3], (DEPTH, DEC_BATCH, POOL_MAX - 1, D_POOL), f32),
        "norm1_g": 1.0 + 0.02 * n(ks[4], (DEPTH, D_MODEL), f32),
        "w_in": n(ks[5], (DEPTH, D_MODEL, D_IN), f32) * D_MODEL ** -0.5,
        "b_in": 0.02 * n(ks[6], (DEPTH, D_IN), f32),
        "conv_w": n(ks[7], (DEPTH, CONV_W, D_CONV), f32) * CONV_W ** -0.5,
        "conv_b": 0.02 * n(ks[8], (DEPTH, D_CONV), f32),
        "ln_g": 1.0 + 0.02 * n(ks[9], (DEPTH, D_CONV), f32),
        "ln_b": 0.02 * n(ks[10], (DEPTH, D_CONV), f32),
        "pool_w": n(ks[11], (DEPTH, N_POOL_GROUPS, POOL_GROUP_W, POOL_GROUP_W), f32) * POOL_GROUP_W ** -0.5,
        "pool_scale": 1.0 + 0.1 * n(ks[12], (DEPTH, D_POOL), f32),
        "w_out": n(ks[13], (DEPTH, D_MIX, D_MODEL), f32) * D_MIX ** -0.5,
        "norm2_g": 1.0 + 0.02 * n(ks[14], (DEPTH, D_MODEL), f32),
        "w_ff1": n(ks[15], (DEPTH, D_MODEL, D_FF), f32) * D_MODEL ** -0.5,
        "w_ff2": n(ks[16], (DEPTH, D_FF, D_MODEL), f32) * D_FF ** -0.5,
        "norm_f": 1.0 + 0.02 * n(ks[17], (D_MODEL,), f32),
    }


def reference(x_prompt, x_sample, state_conv, state_pool, norm1_g, w_in, b_in, conv_w, conv_b,
              ln_g, ln_b, pool_w, pool_scale, w_out, norm2_g, w_ff1, w_ff2, norm_f):
    xp, xs = x_prompt, x_sample
    conv_p, pool_p, conv_s, pool_s = [], [], [], []
    zc = jnp.zeros((BATCH, CONV_W - 1, D_CONV), x_prompt.dtype)
    zp = jnp.zeros((BATCH, POOL_MAX - 1, D_POOL), x_prompt.dtype)
    for l in range(DEPTH):
        params = (norm1_g[l], w_in[l], b_in[l], conv_w[l], conv_b[l], ln_g[l], ln_b[l],
                  pool_w[l], pool_scale[l], w_out[l], norm2_g[l], w_ff1[l], w_ff2[l])
        xp, cp, pp = trunk_layer(xp, zc, zp, 0, *params)
        xs, cs_, ps_ = trunk_layer(xs, state_conv[l], state_pool[l], PAST_LEN, *params)
        conv_p.append(cp); pool_p.append(pp); conv_s.append(cs_); pool_s.append(ps_)
    y_prompt = rmsnorm(xp, norm_f)
    y_sample = rmsnorm(xs, norm_f)
    new_conv_prompt = jnp.stack(conv_p, axis=0)
    new_pool_prompt = jnp.stack(pool_p, axis=0)
    new_conv_sample = jnp.stack(conv_s, axis=0)
    new_pool_sample = jnp.stack(pool_s, axis=0)
    return (y_prompt, y_sample, new_conv_prompt, new_pool_prompt, new_conv_sample, new_pool_sample)
```

```python
import functools

import jax
import jax.numpy as jnp
from jax import lax
from jax.experimental import pallas as pl
from jax.experimental.pallas import tpu as pltpu

F32 = jnp.float32
BF16 = jnp.bfloat16

D_MODEL = 2048
BATCH = 4
SEQ = 2048
DEPTH = 2
DEC_BATCH = 128
DEC_SEQ = 4
PAST_LEN = 16384
D_CONV = 1024
D_POOL = 1024
N_POOL_GROUPS = 4
POOL_GROUP_W = D_POOL // N_POOL_GROUPS
POOL_WINDOWS = (2, 4, 8, 16)
POOL_MAX = 16
CONV_W = 31
D_FF = 4 * D_MODEL
D_IN = 2 * D_CONV + D_POOL
EPS = 1e-6

N_PROMPT_ROWS = BATCH * SEQ
N_SAMPLE_ROWS = DEC_BATCH * DEC_SEQ
ROWS = N_PROMPT_ROWS + N_SAMPLE_ROWS

LANES = 128
SUBLANES = 8
N_LANE_TILES = D_CONV // LANES
VMEM_LIMIT = 56 * 1024 * 1024

TM = 512
TF = 1024
TT = 512
RT = 4
SB = 32
CONV_HIST = 32
POOL_HIST = 16


def _rms(x, g):
    return x * lax.rsqrt(jnp.mean(x * x, axis=-1, keepdims=True) + EPS) * g


def _ln_swish(x, g, b):
    mu = jnp.mean(x, axis=-1, keepdims=True)
    xc = x - mu
    var = jnp.mean(xc * xc, axis=-1, keepdims=True)
    y = xc * lax.rsqrt(var + EPS) * g + b
    return y * jax.nn.sigmoid(y)


def _in_proj_kernel(x_ref, g_ref, w_ref, b_ref, u_ref, z_ref):
    h = _rms(x_ref[...], g_ref[...]).astype(BF16)

    def proj(lo, hi):
        return jnp.dot(h, w_ref[:, lo:hi], preferred_element_type=F32) + b_ref[:, lo:hi]

    a = proj(0, D_CONV)
    gate = proj(D_CONV, 2 * D_CONV)
    u_ref[...] = a * jax.nn.sigmoid(gate)
    z_ref[...] = proj(2 * D_CONV, D_IN)


def _in_proj(x, g, w, b):
    return pl.pallas_call(
        _in_proj_kernel,
        grid=(ROWS // TM,),
        in_specs=[
            pl.BlockSpec((TM, D_MODEL), lambda i: (i, 0)),
            pl.BlockSpec((1, D_MODEL), lambda i: (0, 0)),
            pl.BlockSpec((D_MODEL, D_IN), lambda i: (0, 0), pipeline_mode=pl.Buffered(1)),
            pl.BlockSpec((1, D_IN), lambda i: (0, 0)),
        ],
        out_specs=[
            pl.BlockSpec((TM, D_CONV), lambda i: (i, 0)),
            pl.BlockSpec((TM, D_POOL), lambda i: (i, 0)),
        ],
        out_shape=[
            jax.ShapeDtypeStruct((ROWS, D_CONV), F32),
            jax.ShapeDtypeStruct((ROWS, D_POOL), F32),
        ],
        compiler_params=pltpu.CompilerParams(
            dimension_semantics=("arbitrary",), vmem_limit_bytes=VMEM_LIMIT),
        name="in_proj",
    )(x, g, w, b)


def _lane(L):
    return slice(L * LANES, (L + 1) * LANES)


def _mix_prompt_kernel(u_ref, z_ref, cw_ref, cb_ref, lng_ref, lnb_ref, pw_ref, ps_ref,
                       mix_ref, sc_ref, sp_ref, uext2, zext2, conv_sc, d_sc):
    i = pl.program_id(1)

    def rows2(r):
        return pl.ds(2 * r, SUBLANES, stride=2)

    @pl.when(i == 0)
    def _():
        zero = jnp.zeros((SUBLANES, LANES), F32)
        for L in range(N_LANE_TILES):
            for r in range(0, CONV_HIST, SUBLANES):
                uext2[L, rows2(r), :] = zero
            for r in range(0, POOL_HIST, SUBLANES):
                zext2[L, rows2(r), :] = zero

    @pl.when(i > 0)
    def _():
        for L in range(N_LANE_TILES):
            for r in range(0, CONV_HIST, SUBLANES):
                uext2[L, rows2(r), :] = uext2[L, rows2(TT + r), :]
            for r in range(0, POOL_HIST, SUBLANES):
                zext2[L, rows2(r), :] = zext2[L, rows2(TT + r), :]

    def stage(c, carry):
        r0 = pl.multiple_of(c * SUBLANES, SUBLANES)
        for L in range(N_LANE_TILES):
            uext2[L, rows2(r0 + CONV_HIST), :] = u_ref[pl.ds(r0, SUBLANES), _lane(L)]
            zext2[L, rows2(r0 + POOL_HIST), :] = z_ref[pl.ds(r0, SUBLANES), _lane(L)]
        return carry

    lax.fori_loop(0, TT // SUBLANES, stage, 0)

    step_rows = SUBLANES * RT
    n_steps = TT // step_rows

    for L in range(N_LANE_TILES):
        def conv_body(c, carry, L=L):
            r0 = pl.multiple_of(c * step_rows, step_rows)
            bias = cb_ref[0:1, _lane(L)]
            accs = [jnp.broadcast_to(bias, (SUBLANES, LANES)) for _ in range(RT)]
            for k in range(CONV_W):
                wk = cw_ref[k:k + 1, _lane(L)]
                for j in range(RT):
                    r = r0 + SUBLANES * j + k + (CONV_HIST - (CONV_W - 1))
                    accs[j] = accs[j] + wk * uext2[L, rows2(r), :]
            for j in range(RT):
                conv_sc[pl.ds(r0 + SUBLANES * j, SUBLANES), _lane(L)] = accs[j]
            return carry

        lax.fori_loop(0, n_steps, conv_body, 0)

    for g in range(N_POOL_GROUPS):
        win = POOL_WINDOWS[g]
        for L in range(g * N_LANE_TILES // N_POOL_GROUPS, (g + 1) * N_LANE_TILES // N_POOL_GROUPS):
            def pool_body(c, carry, L=L, win=win):
                r0 = pl.multiple_of(c * step_rows, step_rows)
                for j in range(RT):
                    rt = r0 + SUBLANES * j
                    tok = zext2[L, rows2(rt + POOL_HIST), :]
                    s = tok
                    for m in range(1, win):
                        s = s + zext2[L, rows2(rt + POOL_HIST - m), :]
                    pos = i * TT + rt + lax.broadcasted_iota(jnp.int32, (SUBLANES, LANES), 0)
                    cnt = jnp.minimum(win, pos + 1).astype(F32)
                    d_sc[pl.ds(rt, SUBLANES), _lane(L)] = s / cnt - tok
                return carry

            lax.fori_loop(0, n_steps, pool_body, 0)

    half = TT // 2
    for rc in range(0, TT, half):
        c = _ln_swish(conv_sc[rc:rc + half, :], lng_ref[...], lnb_ref[...])
        mix_ref[rc:rc + half, 0:D_CONV] = c.astype(BF16)
        d = d_sc[rc:rc + half, :].astype(BF16)
        for g in range(N_POOL_GROUPS):
            cols = slice(g * POOL_GROUP_W, (g + 1) * POOL_GROUP_W)
            p = jnp.dot(d[:, cols], pw_ref[g], preferred_element_type=F32) * ps_ref[:, cols]
            mix_ref[rc:rc + half, D_CONV + g * POOL_GROUP_W:D_CONV + (g + 1) * POOL_GROUP_W] = p.astype(BF16)

    @pl.when(i == pl.num_programs(1) - 1)
    def _():
        sc_ref[...] = u_ref[TT - (CONV_W - 1):TT, :]
        sp_ref[...] = z_ref[TT - (POOL_MAX - 1):TT, :]


def _mix_prompt(u, z, cw, cb, lng, lnb, pw, ps):
    n_t = SEQ // TT
    row_map = lambda b, i: (b * n_t + i, 0)
    const2 = lambda b, i: (0, 0)
    return pl.pallas_call(
        _mix_prompt_kernel,
        grid=(BATCH, n_t),
        in_specs=[
            pl.BlockSpec((TT, D_CONV), row_map),
            pl.BlockSpec((TT, D_POOL), row_map),
            pl.BlockSpec((CONV_W, D_CONV), const2),
            pl.BlockSpec((1, D_CONV), const2),
            pl.BlockSpec((1, D_CONV), const2),
            pl.BlockSpec((1, D_CONV), const2),
            pl.BlockSpec((N_POOL_GROUPS, POOL_GROUP_W, POOL_GROUP_W), lambda b, i: (0, 0, 0)),
            pl.BlockSpec((1, D_POOL), const2),
        ],
        out_specs=[
            pl.BlockSpec((TT, D_MODEL), row_map),
            pl.BlockSpec((None, CONV_W - 1, D_CONV), lambda b, i: (b, 0, 0)),
            pl.BlockSpec((None, POOL_MAX - 1, D_POOL), lambda b, i: (b, 0, 0)),
        ],
        out_shape=[
            jax.ShapeDtypeStruct((ROWS, D_MODEL), BF16),
            jax.ShapeDtypeStruct((BATCH, CONV_W - 1, D_CONV), F32),
            jax.ShapeDtypeStruct((BATCH, POOL_MAX - 1, D_POOL), F32),
        ],
        scratch_shapes=[
            pltpu.VMEM((N_LANE_TILES, 2 * (TT + CONV_HIST), LANES), F32),
            pltpu.VMEM((N_LANE_TILES, 2 * (TT + POOL_HIST), LANES), F32),
            pltpu.VMEM((TT, D_CONV), F32),
            pltpu.VMEM((TT, D_POOL), F32),
        ],
        compiler_params=pltpu.CompilerParams(
            dimension_semantics=("arbitrary", "arbitrary"), vmem_limit_bytes=VMEM_LIMIT),
        name="mix_prompt",
    )(u, z, cw, cb, lng, lnb, pw, ps)


def _mix_sample_kernel(u_ref, z_ref, hc_ref, hp_ref, cw_ref, cb_ref, lng_ref, lnb_ref, pw_ref, ps_ref,
                       mix_in_ref, mix_ref, nc_ref, np_ref):
    del mix_in_ref
    hc = CONV_W - 1
    hp = POOL_MAX - 1

    def uext(j):
        return hc_ref[j] if j < hc else u_ref[j - hc]

    def zext(j):
        return hp_ref[j] if j < hp else z_ref[j - hp]

    for t in range(DEC_SEQ):
        acc = jnp.broadcast_to(cb_ref[...], (SB, D_CONV))
        for k in range(CONV_W):
            acc = acc + cw_ref[k:k + 1, :] * uext(t + k)
        mix_ref[t, :, 0:D_CONV] = _ln_swish(acc, lng_ref[...], lnb_ref[...]).astype(BF16)

        for g in range(N_POOL_GROUPS):
            win = POOL_WINDOWS[g]
            cols = slice(g * POOL_GROUP_W, (g + 1) * POOL_GROUP_W)
            tok = zext(hp + t)[:, cols]
            s = tok
            for m in range(1, win):
                s = s + zext(hp + t - m)[:, cols]
            cnt = float(min(win, PAST_LEN + t + 1))
            d = (s / cnt - tok).astype(BF16)
            p = jnp.dot(d, pw_ref[g], preferred_element_type=F32) * ps_ref[:, cols]
            mix_ref[t, :, D_CONV + g * POOL_GROUP_W:D_CONV + (g + 1) * POOL_GROUP_W] = p.astype(BF16)

    for j in range(hc):
        nc_ref[j] = uext(j + DEC_SEQ)
    for j in range(hp):
        np_ref[j] = zext(j + DEC_SEQ)


def _mix_sample(u3, z3, hist_c, hist_p, cw, cb, lng, lnb, pw, ps, mix3):
    t_blk = N_PROMPT_ROWS // (DEC_BATCH * DEC_SEQ)
    tok_map = lambda s: (t_blk, s, 0)
    hist_map = lambda s: (0, s, 0)
    const2 = lambda s: (0, 0)
    return pl.pallas_call(
        _mix_sample_kernel,
        grid=(DEC_BATCH // SB,),
        in_specs=[
            pl.BlockSpec((DEC_SEQ, SB, D_CONV), tok_map),
            pl.BlockSpec((DEC_SEQ, SB, D_POOL), tok_map),
            pl.BlockSpec((CONV_W - 1, SB, D_CONV), hist_map),
            pl.BlockSpec((POOL_MAX - 1, SB, D_POOL), hist_map),
            pl.BlockSpec((CONV_W, D_CONV), const2),
            pl.BlockSpec((1, D_CONV), const2),
            pl.BlockSpec((1, D_CONV), const2),
            pl.BlockSpec((1, D_CONV), const2),
            pl.BlockSpec((N_POOL_GROUPS, POOL_GROUP_W, POOL_GROUP_W), lambda s: (0, 0, 0)),
            pl.BlockSpec((1, D_POOL), const2),
            pl.BlockSpec(memory_space=pl.ANY),
        ],
        out_specs=[
            pl.BlockSpec((DEC_SEQ, SB, D_MODEL), tok_map),
            pl.BlockSpec((CONV_W - 1, SB, D_CONV), hist_map),
            pl.BlockSpec((POOL_MAX - 1, SB, D_POOL), hist_map),
        ],
        out_shape=[
            jax.ShapeDtypeStruct(mix3.shape, BF16),
            jax.ShapeDtypeStruct((CONV_W - 1, DEC_BATCH, D_CONV), F32),
            jax.ShapeDtypeStruct((POOL_MAX - 1, DEC_BATCH, D_POOL), F32),
        ],
        input_output_aliases={10: 0},
        compiler_params=pltpu.CompilerParams(
            dimension_semantics=("arbitrary",), vmem_limit_bytes=VMEM_LIMIT),
        name="mix_sample",
    )(u3, z3, hist_c, hist_p, cw, cb, lng, lnb, pw, ps, mix3)


def _out_ffn_kernel(x_ref, mix_ref, wo_ref, g2_ref, w1_ref, w2_ref, gf_ref, o_ref, h2_ref, *, final_norm):
    j = pl.program_id(1)

    @pl.when(j == 0)
    def _():
        x1 = x_ref[...] + jnp.dot(mix_ref[...], wo_ref[...], preferred_element_type=F32)
        o_ref[...] = x1
        h2_ref[...] = _rms(x1, g2_ref[...]).astype(BF16)

    f = jnp.dot(h2_ref[...], w1_ref[...], preferred_element_type=F32)
    f = jnp.square(jnp.maximum(f, 0.0)).astype(BF16)
    o_ref[...] += jnp.dot(f, w2_ref[...], preferred_element_type=F32)

    if final_norm:
        @pl.when(j == pl.num_programs(1) - 1)
        def _():
            o_ref[...] = _rms(o_ref[...], gf_ref[...])


def _out_ffn(x, mix, wo, g2, w1, w2, gf, final_norm):
    return pl.pallas_call(
        functools.partial(_out_ffn_kernel, final_norm=final_norm),
        grid=(ROWS // TM, D_FF // TF),
        in_specs=[
            pl.BlockSpec((TM, D_MODEL), lambda i, j: (i, 0)),
            pl.BlockSpec((TM, D_MODEL), lambda i, j: (i, 0)),
            pl.BlockSpec((D_MODEL, D_MODEL), lambda i, j: (0, 0), pipeline_mode=pl.Buffered(1)),
            pl.BlockSpec((1, D_MODEL), lambda i, j: (0, 0)),
            pl.BlockSpec((D_MODEL, TF), lambda i, j: (0, j)),
            pl.BlockSpec((TF, D_MODEL), lambda i, j: (j, 0)),
            pl.BlockSpec((1, D_MODEL), lambda i, j: (0, 0)),
        ],
        out_specs=pl.BlockSpec((TM, D_MODEL), lambda i, j: (i, 0)),
        out_shape=jax.ShapeDtypeStruct((ROWS, D_MODEL), F32),
        scratch_shapes=[pltpu.VMEM((TM, D_MODEL), BF16)],
        compiler_params=pltpu.CompilerParams(
            dimension_semantics=("arbitrary", "arbitrary"), vmem_limit_bytes=VMEM_LIMIT),
        name="out_ffn",
    )(x, mix, wo, g2, w1, w2, gf)


def kernel(x_prompt, x_sample, state_conv, state_pool, norm1_g, w_in, b_in, conv_w, conv_b, ln_g, ln_b, pool_w, pool_scale, w_out, norm2_g, w_ff1, w_ff2, norm_f):
    x = jnp.concatenate([
        x_prompt.reshape(N_PROMPT_ROWS, D_MODEL),
        x_sample.transpose(1, 0, 2).reshape(N_SAMPLE_ROWS, D_MODEL)], axis=0)
    hist_c = state_conv.transpose(0, 2, 1, 3)
    hist_p = state_pool.transpose(0, 2, 1, 3)
    w_in_b = w_in.astype(BF16)
    w_out_b = w_out.astype(BF16)
    w_ff1_b = w_ff1.astype(BF16)
    w_ff2_b = w_ff2.astype(BF16)
    pool_w_b = pool_w.astype(BF16)

    conv_p, pool_p, conv_s, pool_s = [], [], [], []
    for l in range(DEPTH):
        row = lambda a: a[l][None, :]
        u, z = _in_proj(x, row(norm1_g), w_in_b[l], row(b_in))
        mixer_w = (conv_w[l], row(conv_b), row(ln_g), row(ln_b), pool_w_b[l], row(pool_scale))
        mix, cp, pp = _mix_prompt(u, z, *mixer_w)
        as3 = lambda a: a.reshape(ROWS // DEC_BATCH, DEC_BATCH, a.shape[-1])
        mix3, cs, ps = _mix_sample(as3(u), as3(z), hist_c[l], hist_p[l], *mixer_w, as3(mix))
        x = _out_ffn(x, mix3.reshape(ROWS, D_MODEL), w_out_b[l], row(norm2_g), w_ff1_b[l], w_ff2_b[l],
                     norm_f[None, :], final_norm=(l == DEPTH - 1))
        conv_p.append(cp)
        pool_p.append(pp)
        conv_s.append(cs.transpose(1, 0, 2))
        pool_s.append(ps.transpose(1, 0, 2))

    y_prompt = x[:N_PROMPT_ROWS].reshape(BATCH, SEQ, D_MODEL)
    y_sample = x[N_PROMPT_ROWS:].reshape(DEC_SEQ, DEC_BATCH, D_MODEL).transpose(1, 0, 2)
    return (y_prompt, y_sample, jnp.stack(conv_p), jnp.stack(pool_p), jnp.stack(conv_s), jnp.stack(pool_s))
```

```python
import functools

import jax
import jax.numpy as jnp
from jax import lax
from jax.experimental import pallas as pl
from jax.experimental.pallas import tpu as pltpu

F32 = jnp.float32
BF16 = jnp.bfloat16

D_MODEL = 2048
BATCH = 4
SEQ = 2048
DEPTH = 2
DEC_BATCH = 128
DEC_SEQ = 4
PAST_LEN = 16384
D_CONV = 1024
D_POOL = 1024
N_POOL_GROUPS = 4
POOL_GROUP_W = D_POOL // N_POOL_GROUPS
POOL_WINDOWS = (2, 4, 8, 16)
POOL_MAX = 16
CONV_W = 31
D_FF = 4 * D_MODEL
D_IN = 2 * D_CONV + D_POOL
EPS = 1e-6

CONV_H = CONV_W - 1
POOL_H = POOL_MAX - 1

LANES = 128
SUBLANES = 8
N_LANE_TILES = D_CONV // LANES
VMEM_LIMIT = 56 * 1024 * 1024

TM = 512
TF = 1024
TT = 512
RT = 8
CONV_STG_H = 32
POOL_STG_H = 16

SB = 16
CONV_REG = 40
POOL_REG = 24
assert CONV_REG >= CONV_H + SUBLANES and POOL_REG >= POOL_H + SUBLANES
assert DEC_SEQ * 2 == SUBLANES


def _lane(L):
    return slice(L * LANES, (L + 1) * LANES)


def _rows2(r, n=SUBLANES):
    return pl.ds(2 * r, n, stride=2)


def _rms(x, g):
    return x * lax.rsqrt(jnp.mean(x * x, axis=-1, keepdims=True) + EPS) * g


def _ln_swish(x, g, b):
    mu = jnp.mean(x, axis=-1, keepdims=True)
    xc = x - mu
    var = jnp.mean(xc * xc, axis=-1, keepdims=True)
    y = xc * lax.rsqrt(var + EPS) * g + b
    return y * jax.nn.sigmoid(y)


def _mix_epilogue(conv, d, lng_ref, lnb_ref, pw_ref, ps_ref, mix_ref, rows):
    c = _ln_swish(conv, lng_ref[...], lnb_ref[...])
    mix_ref[rows, 0:D_CONV] = c.astype(BF16)
    d = d.astype(BF16)
    for g in range(N_POOL_GROUPS):
        cols = slice(g * POOL_GROUP_W, (g + 1) * POOL_GROUP_W)
        p = jnp.dot(d[:, cols], pw_ref[g], preferred_element_type=F32) * ps_ref[:, cols]
        mix_ref[rows, D_CONV + g * POOL_GROUP_W:D_CONV + (g + 1) * POOL_GROUP_W] = p.astype(BF16)


def _in_proj_kernel(x_ref, g_ref, w_ref, b_ref, u_ref, z_ref):
    h = _rms(x_ref[...], g_ref[...]).astype(BF16)

    def proj(lo, hi):
        return jnp.dot(h, w_ref[:, lo:hi], preferred_element_type=F32) + b_ref[:, lo:hi]

    a = proj(0, D_CONV)
    gate = proj(D_CONV, 2 * D_CONV)
    u_ref[...] = a * jax.nn.sigmoid(gate)
    z_ref[...] = proj(2 * D_CONV, D_IN)


def _in_proj(x, g, w, b):
    rows = x.shape[0]
    return pl.pallas_call(
        _in_proj_kernel,
        grid=(rows // TM,),
        in_specs=[
            pl.BlockSpec((TM, D_MODEL), lambda i: (i, 0)),
            pl.BlockSpec((1, D_MODEL), lambda i: (0, 0)),
            pl.BlockSpec((D_MODEL, D_IN), lambda i: (0, 0), pipeline_mode=pl.Buffered(1)),
            pl.BlockSpec((1, D_IN), lambda i: (0, 0)),
        ],
        out_specs=[
            pl.BlockSpec((TM, D_CONV), lambda i: (i, 0)),
            pl.BlockSpec((TM, D_POOL), lambda i: (i, 0)),
        ],
        out_shape=[
            jax.ShapeDtypeStruct((rows, D_CONV), F32),
            jax.ShapeDtypeStruct((rows, D_POOL), F32),
        ],
        compiler_params=pltpu.CompilerParams(
            dimension_semantics=("arbitrary",), vmem_limit_bytes=VMEM_LIMIT),
        name="in_proj",
    )(x, g, w, b)


def _mix_prompt_kernel(u_ref, z_ref, cw_ref, cb_ref, lng_ref, lnb_ref, pw_ref, ps_ref,
                       mix_ref, sc_ref, sp_ref, uext2, zext2, conv_sc, d_sc):
    i = pl.program_id(1)

    @pl.when(i == 0)
    def _():
        zero = jnp.zeros((SUBLANES, LANES), F32)
        for L in range(N_LANE_TILES):
            for r in range(0, CONV_STG_H, SUBLANES):
                uext2[L, _rows2(r), :] = zero
            for r in range(0, POOL_STG_H, SUBLANES):
                zext2[L, _rows2(r), :] = zero

    @pl.when(i > 0)
    def _():
        for L in range(N_LANE_TILES):
            for r in range(0, CONV_STG_H, SUBLANES):
                uext2[L, _rows2(r), :] = uext2[L, _rows2(TT + r), :]
            for r in range(0, POOL_STG_H, SUBLANES):
                zext2[L, _rows2(r), :] = zext2[L, _rows2(TT + r), :]

    def stage(c, carry):
        r0 = pl.multiple_of(c * SUBLANES, SUBLANES)
        for L in range(N_LANE_TILES):
            uext2[L, _rows2(r0 + CONV_STG_H), :] = u_ref[pl.ds(r0, SUBLANES), _lane(L)]
            zext2[L, _rows2(r0 + POOL_STG_H), :] = z_ref[pl.ds(r0, SUBLANES), _lane(L)]
        return carry

    lax.fori_loop(0, TT // SUBLANES, stage, 0)

    step_rows = SUBLANES * RT
    n_steps = TT // step_rows

    for L in range(N_LANE_TILES):
        def conv_body(c, carry, L=L):
            r0 = pl.multiple_of(c * step_rows, step_rows)
            bias = cb_ref[0:1, _lane(L)]
            accs = [jnp.broadcast_to(bias, (SUBLANES, LANES)) for _ in range(RT)]
            for k in range(CONV_W):
                wk = cw_ref[k:k + 1, _lane(L)]
                for j in range(RT):
                    r = r0 + SUBLANES * j + k + (CONV_STG_H - CONV_H)
                    accs[j] = accs[j] + wk * uext2[L, _rows2(r), :]
            for j in range(RT):
                conv_sc[pl.ds(r0 + SUBLANES * j, SUBLANES), _lane(L)] = accs[j]
            return carry

        lax.fori_loop(0, n_steps, conv_body, 0)

    def window_sum(L, win, rt):
        tok = zext2[L, _rows2(rt + POOL_STG_H), :]
        s = tok
        for m in range(1, win):
            s = s + zext2[L, _rows2(rt + POOL_STG_H - m), :]
        return s, tok

    tiles_per_group = N_LANE_TILES // N_POOL_GROUPS
    for L in range(N_LANE_TILES):
        win = POOL_WINDOWS[L // tiles_per_group]

        def pool_body(c, carry, L=L, win=win):
            r0 = pl.multiple_of(c * step_rows, step_rows)
            for j in range(RT):
                rt = r0 + SUBLANES * j
                s, tok = window_sum(L, win, rt)
                d_sc[pl.ds(rt, SUBLANES), _lane(L)] = s / float(win) - tok
            return carry

        lax.fori_loop(0, n_steps, pool_body, 0)

    @pl.when(i == 0)
    def _():
        for L in range(N_LANE_TILES):
            win = POOL_WINDOWS[L // tiles_per_group]
            for rt in range(0, POOL_STG_H, SUBLANES):
                s, tok = window_sum(L, win, rt)
                pos = rt + lax.broadcasted_iota(jnp.int32, (SUBLANES, LANES), 0)
                cnt = jnp.minimum(win, pos + 1).astype(F32)
                d_sc[rt:rt + SUBLANES, _lane(L)] = s / cnt - tok

    half = TT // 2
    for rc in range(0, TT, half):
        rows = slice(rc, rc + half)
        _mix_epilogue(conv_sc[rows, :], d_sc[rows, :], lng_ref, lnb_ref, pw_ref, ps_ref, mix_ref, rows)

    @pl.when(i == pl.num_programs(1) - 1)
    def _():
        sc_ref[...] = u_ref[TT - CONV_H:TT, :]
        sp_ref[...] = z_ref[TT - POOL_H:TT, :]


def _mix_prompt(u, z, cw, cb, lng, lnb, pw, ps):
    n_t = SEQ // TT
    row_map = lambda b, i: (b * n_t + i, 0)
    const2 = lambda b, i: (0, 0)
    return pl.pallas_call(
        _mix_prompt_kernel,
        grid=(BATCH, n_t),
        in_specs=[
            pl.BlockSpec((TT, D_CONV), row_map),
            pl.BlockSpec((TT, D_POOL), row_map),
            pl.BlockSpec((CONV_W, D_CONV), const2),
            pl.BlockSpec((1, D_CONV), const2),
            pl.BlockSpec((1, D_CONV), const2),
            pl.BlockSpec((1, D_CONV), const2),
            pl.BlockSpec((N_POOL_GROUPS, POOL_GROUP_W, POOL_GROUP_W), lambda b, i: (0, 0, 0)),
            pl.BlockSpec((1, D_POOL), const2),
        ],
        out_specs=[
            pl.BlockSpec((TT, D_MODEL), row_map),
            pl.BlockSpec((None, CONV_H, D_CONV), lambda b, i: (b, 0, 0)),
            pl.BlockSpec((None, POOL_H, D_POOL), lambda b, i: (b, 0, 0)),
        ],
        out_shape=[
            jax.ShapeDtypeStruct((BATCH * SEQ, D_MODEL), BF16),
            jax.ShapeDtypeStruct((BATCH, CONV_H, D_CONV), F32),
            jax.ShapeDtypeStruct((BATCH, POOL_H, D_POOL), F32),
        ],
        scratch_shapes=[
            pltpu.VMEM((N_LANE_TILES, 2 * (TT + CONV_STG_H), LANES), F32),
            pltpu.VMEM((N_LANE_TILES, 2 * (TT + POOL_STG_H), LANES), F32),
            pltpu.VMEM((TT, D_CONV), F32),
            pltpu.VMEM((TT, D_POOL), F32),
        ],
        compiler_params=pltpu.CompilerParams(
            dimension_semantics=("arbitrary", "arbitrary"), vmem_limit_bytes=VMEM_LIMIT),
        name="mix_prompt",
    )(u, z, cw, cb, lng, lnb, pw, ps)


def _mix_sample_kernel(layer, u_ref, z_ref, hc_ref, hp_ref, cw_ref, cb_ref, lng_ref, lnb_ref, pw_ref, ps_ref,
                       *rest):
    if layer > 0:
        rest = rest[2:]
    mix_ref, nc_ref, np_ref, ustg, zstg, conv_sc, d_sc = rest
    step = pl.program_id(0)

    @pl.when(step == 0)
    def _():
        zero = jnp.zeros((SUBLANES, LANES), F32)
        for L in range(N_LANE_TILES):
            for b in range(SB):
                ustg[L, _rows2(b * CONV_REG + CONV_REG - SUBLANES), :] = zero
                zstg[L, _rows2(b * POOL_REG + POOL_REG - SUBLANES), :] = zero

    row = lax.broadcasted_iota(jnp.int32, (SUBLANES, LANES), 0)
    lower = row < DEC_SEQ
    tiles_per_group = N_LANE_TILES // N_POOL_GROUPS

    def merge(a0, a1):
        return jnp.where(lower, a0, pltpu.roll(a1, DEC_SEQ, axis=0))

    def pair_body(q, carry):
        seqs = (2 * q, 2 * q + 1)
        cbase = [s * CONV_REG for s in seqs]
        pbase = [s * POOL_REG for s in seqs]
        tok_rows = pl.ds(pl.multiple_of(q * SUBLANES, SUBLANES), SUBLANES)
        for L in range(N_LANE_TILES):
            lanes = _lane(L)
            uv = u_ref[tok_rows, lanes]
            zv = z_ref[tok_rows, lanes]
            ustg[L, _rows2(cbase[0] + CONV_H), :] = uv
            ustg[L, _rows2(cbase[1] + CONV_H - DEC_SEQ), :] = uv
            zstg[L, _rows2(pbase[0] + POOL_H), :] = zv
            zstg[L, _rows2(pbase[1] + POOL_H - DEC_SEQ), :] = zv
            for s, cb, pb in zip(seqs, cbase, pbase):
                for r in range(0, CONV_H, SUBLANES):
                    n = min(SUBLANES, CONV_H - r)
                    ustg[L, _rows2(cb + r, n), :] = hc_ref[s, r:r + n, lanes]
                for r in range(0, POOL_H, SUBLANES):
                    n = min(SUBLANES, POOL_H - r)
                    zstg[L, _rows2(pb + r, n), :] = hp_ref[s, r:r + n, lanes]

            for s, cb, pb in zip(seqs, cbase, pbase):
                for r in range(0, CONV_H, SUBLANES):
                    n = min(SUBLANES, CONV_H - r)
                    nc_ref[s, r:r + n, lanes] = ustg[L, _rows2(cb + DEC_SEQ + r, n), :]
                for r in range(0, POOL_H, SUBLANES):
                    n = min(SUBLANES, POOL_H - r)
                    np_ref[s, r:r + n, lanes] = zstg[L, _rows2(pb + DEC_SEQ + r, n), :]

            bias = jnp.broadcast_to(cb_ref[0:1, lanes], (SUBLANES, LANES))
            accs = [bias, bias]
            for k in range(CONV_W):
                wk = cw_ref[k:k + 1, lanes]
                for h in range(2):
                    accs[h] = accs[h] + wk * ustg[L, _rows2(cbase[h] + k), :]
            conv_sc[tok_rows, lanes] = merge(accs[0], accs[1])

            win = POOL_WINDOWS[L // tiles_per_group]
            sums, toks = [], []
            for h in range(2):
                tok = zstg[L, _rows2(pbase[h] + POOL_H), :]
                s_ = tok
                for m in range(1, win):
                    s_ = s_ + zstg[L, _rows2(pbase[h] + POOL_H - m), :]
                sums.append(s_)
                toks.append(tok)
            pos = PAST_LEN + (row & (DEC_SEQ - 1))
            cnt = jnp.minimum(win, pos + 1).astype(F32)
            d_sc[tok_rows, lanes] = merge(sums[0], sums[1]) / cnt - merge(toks[0], toks[1])
        return carry

    lax.fori_loop(0, SB // 2, pair_body, 0)

    rows = slice(0, SB * DEC_SEQ)
    _mix_epilogue(conv_sc[...], d_sc[...], lng_ref, lnb_ref, pw_ref, ps_ref, mix_ref, rows)


def _mix_sample(layer, u, z, state_conv, state_pool, cw, cb, lng, lnb, pw, ps, prev_states):
    tok_map = lambda s: (s, 0)
    st_map = lambda s: (layer, s, 0, 0)
    const2 = lambda s: (0, 0)
    n_in = 10
    return pl.pallas_call(
        functools.partial(_mix_sample_kernel, layer),
        grid=(DEC_BATCH // SB,),
        in_specs=[
            pl.BlockSpec((SB * DEC_SEQ, D_CONV), tok_map),
            pl.BlockSpec((SB * DEC_SEQ, D_POOL), tok_map),
            pl.BlockSpec((None, SB, CONV_H, D_CONV), st_map),
            pl.BlockSpec((None, SB, POOL_H, D_POOL), st_map),
            pl.BlockSpec((CONV_W, D_CONV), const2),
            pl.BlockSpec((1, D_CONV), const2),
            pl.BlockSpec((1, D_CONV), const2),
            pl.BlockSpec((1, D_CONV), const2),
            pl.BlockSpec((N_POOL_GROUPS, POOL_GROUP_W, POOL_GROUP_W), lambda s: (0, 0, 0)),
            pl.BlockSpec((1, D_POOL), const2),
        ] + [pl.BlockSpec(memory_space=pl.ANY)] * len(prev_states),
        out_specs=[
            pl.BlockSpec((SB * DEC_SEQ, D_MODEL), tok_map),
            pl.BlockSpec((None, SB, CONV_H, D_CONV), st_map),
            pl.BlockSpec((None, SB, POOL_H, D_POOL), st_map),
        ],
        out_shape=[
            jax.ShapeDtypeStruct((DEC_BATCH * DEC_SEQ, D_MODEL), BF16),
            jax.ShapeDtypeStruct((DEPTH, DEC_BATCH, CONV_H, D_CONV), F32),
            jax.ShapeDtypeStruct((DEPTH, DEC_BATCH, POOL_H, D_POOL), F32),
        ],
        scratch_shapes=[
            pltpu.VMEM((N_LANE_TILES, 2 * SB * CONV_REG, LANES), F32),
            pltpu.VMEM((N_LANE_TILES, 2 * SB * POOL_REG, LANES), F32),
            pltpu.VMEM((SB * DEC_SEQ, D_CONV), F32),
            pltpu.VMEM((SB * DEC_SEQ, D_POOL), F32),
        ],
        input_output_aliases={n_in + k: 1 + k for k in range(len(prev_states))},
        compiler_params=pltpu.CompilerParams(
            dimension_semantics=("arbitrary",), vmem_limit_bytes=VMEM_LIMIT),
        name="mix_sample",
    )(u, z, state_conv, state_pool, cw, cb, lng, lnb, pw, ps, *prev_states)


def _out_ffn_kernel(x_ref, mix_ref, wo_ref, g2_ref, w1_ref, w2_ref, gf_ref, o_ref, h2_ref, *, final_norm):
    j = pl.program_id(1)

    @pl.when(j == 0)
    def _():
        x1 = x_ref[...] + jnp.dot(mix_ref[...], wo_ref[...], preferred_element_type=F32)
        o_ref[...] = x1
        h2_ref[...] = _rms(x1, g2_ref[...]).astype(BF16)

    f = jnp.dot(h2_ref[...], w1_ref[...], preferred_element_type=F32)
    f = jnp.square(jnp.maximum(f, 0.0)).astype(BF16)
    o_ref[...] += jnp.dot(f, w2_ref[...], preferred_element_type=F32)

    if final_norm:
        @pl.when(j == pl.num_programs(1) - 1)
        def _():
            o_ref[...] = _rms(o_ref[...], gf_ref[...])


def _out_ffn(x, mix, wo, g2, w1, w2, gf, final_norm):
    rows = x.shape[0]
    return pl.pallas_call(
        functools.partial(_out_ffn_kernel, final_norm=final_norm),
        grid=(rows // TM, D_FF // TF),
        in_specs=[
            pl.BlockSpec((TM, D_MODEL), lambda i, j: (i, 0)),
            pl.BlockSpec((TM, D_MODEL), lambda i, j: (i, 0)),
            pl.BlockSpec((D_MODEL, D_MODEL), lambda i, j: (0, 0), pipeline_mode=pl.Buffered(1)),
            pl.BlockSpec((1, D_MODEL), lambda i, j: (0, 0)),
            pl.BlockSpec((D_MODEL, TF), lambda i, j: (0, j)),
            pl.BlockSpec((TF, D_MODEL), lambda i, j: (j, 0)),
            pl.BlockSpec((1, D_MODEL), lambda i, j: (0, 0)),
        ],
        out_specs=pl.BlockSpec((TM, D_MODEL), lambda i, j: (i, 0)),
        out_shape=jax.ShapeDtypeStruct((rows, D_MODEL), F32),
        scratch_shapes=[pltpu.VMEM((TM, D_MODEL), BF16)],
        compiler_params=pltpu.CompilerParams(
            dimension_semantics=("arbitrary", "arbitrary"), vmem_limit_bytes=VMEM_LIMIT),
        name="out_ffn",
    )(x, mix, wo, g2, w1, w2, gf)


def kernel(x_prompt, x_sample, state_conv, state_pool, norm1_g, w_in, b_in, conv_w, conv_b, ln_g, ln_b, pool_w, pool_scale, w_out, norm2_g, w_ff1, w_ff2, norm_f):
    xp = x_prompt.reshape(BATCH * SEQ, D_MODEL)
    xs = x_sample.reshape(DEC_BATCH * DEC_SEQ, D_MODEL)
    w_in_b = w_in.astype(BF16)
    w_out_b = w_out.astype(BF16)
    w_ff1_b = w_ff1.astype(BF16)
    w_ff2_b = w_ff2.astype(BF16)
    pool_w_b = pool_w.astype(BF16)

    conv_p, pool_p = [], []
    sample_states = ()
    for l in range(DEPTH):
        row = lambda a: a[l][None, :]
        mixer_w = (conv_w[l], row(conv_b), row(ln_g), row(ln_b), pool_w_b[l], row(pool_scale))
        ffn_w = (w_out_b[l], row(norm2_g), w_ff1_b[l], w_ff2_b[l], norm_f[None, :])
        final = l == DEPTH - 1

        up, zp = _in_proj(xp, row(norm1_g), w_in_b[l], row(b_in))
        us, zs = _in_proj(xs, row(norm1_g), w_in_b[l], row(b_in))
        mix_p, cp, pp = _mix_prompt(up, zp, *mixer_w)
        mix_s, *sample_states = _mix_sample(l, us, zs, state_conv, state_pool, *mixer_w, sample_states)
        xp = _out_ffn(xp, mix_p, *ffn_w, final_norm=final)
        xs = _out_ffn(xs, mix_s, *ffn_w, final_norm=final)
        conv_p.append(cp)
        pool_p.append(pp)

    conv_s, pool_s = sample_states
    return (xp.reshape(BATCH, SEQ, D_MODEL), xs.reshape(DEC_BATCH, DEC_SEQ, D_MODEL),
            jnp.stack(conv_p), jnp.stack(pool_p), conv_s, pool_s)
```

```python
import functools

import jax
import jax.numpy as jnp
from jax import lax
from jax.experimental import pallas as pl
from jax.experimental.pallas import tpu as pltpu

F32 = jnp.float32
BF16 = jnp.bfloat16

D_MODEL = 2048
BATCH = 4
SEQ = 2048
DEPTH = 2
DEC_BATCH = 128
DEC_SEQ = 4
PAST_LEN = 16384
D_CONV = 1024
D_POOL = 1024
N_POOL_GROUPS = 4
POOL_GROUP_W = D_POOL // N_POOL_GROUPS
POOL_WINDOWS = (2, 4, 8, 16)
POOL_MAX = 16
CONV_W = 31
D_FF = 4 * D_MODEL
D_IN = 2 * D_CONV + D_POOL
EPS = 1e-6

CONV_H = CONV_W - 1
POOL_H = POOL_MAX - 1

LANES = 128
SUBLANES = 8
BF16_ROWS = 16
N_LANE_TILES = D_CONV // LANES
VMEM_LIMIT = 56 * 1024 * 1024

TM = 512
TF = 1024
TT = 512
RT = 8
CONV_STG_H = 32
POOL_STG_H = 16
SB = 32


def _lane(L):
    return slice(L * LANES, (L + 1) * LANES)


def _rows2(r, n=SUBLANES):
    return pl.ds(2 * r, n, stride=2)


def _rms(x, g):
    return x * lax.rsqrt(jnp.mean(x * x, axis=-1, keepdims=True) + EPS) * g


def _ln_swish(x, g, b):
    mu = jnp.mean(x, axis=-1, keepdims=True)
    xc = x - mu
    var = jnp.mean(xc * xc, axis=-1, keepdims=True)
    y = xc * lax.rsqrt(var + EPS) * g + b
    return y * jax.nn.sigmoid(y)


def _group_map(d, g, pw_ref, ps_ref):
    cols = slice(g * POOL_GROUP_W, (g + 1) * POOL_GROUP_W)
    return jnp.dot(d.astype(BF16), pw_ref[g], preferred_element_type=F32) * ps_ref[:, cols]


def _cast_specs(jobs, n_steps, step_of):
    in_specs, out_specs, out_shapes = [], [], []
    for w, layer in jobs:
        _, r, c = w.shape
        rows = r // n_steps
        assert rows * n_steps == r and rows % BF16_ROWS == 0
        in_specs.append(pl.BlockSpec(
            (None, rows, c), lambda *ids, layer=layer: (layer, step_of(*ids), 0)))
        out_specs.append(pl.BlockSpec((rows, c), lambda *ids: (step_of(*ids), 0)))
        out_shapes.append(jax.ShapeDtypeStruct((r, c), BF16))
    return in_specs, out_specs, out_shapes


def _run_casts(src_refs, dst_refs):
    for src, dst in zip(src_refs, dst_refs):
        dst[...] = src[...].astype(BF16)


def _in_proj_kernel(x_ref, g_ref, w_ref, b_ref, u_ref, z_ref):
    h = _rms(x_ref[...], g_ref[...]).astype(BF16)

    def proj(lo, hi):
        return jnp.dot(h, w_ref[:, lo:hi], preferred_element_type=F32) + b_ref[:, lo:hi]

    a = proj(0, D_CONV)
    gate = proj(D_CONV, 2 * D_CONV)
    u_ref[...] = a * jax.nn.sigmoid(gate)
    z_ref[...] = proj(2 * D_CONV, D_IN)


def _in_proj(x, g, w, b):
    rows = x.shape[0]
    return pl.pallas_call(
        _in_proj_kernel,
        grid=(rows // TM,),
        in_specs=[
            pl.BlockSpec((TM, D_MODEL), lambda i: (i, 0)),
            pl.BlockSpec((1, D_MODEL), lambda i: (0, 0)),
            pl.BlockSpec((D_MODEL, D_IN), lambda i: (0, 0), pipeline_mode=pl.Buffered(1)),
            pl.BlockSpec((1, D_IN), lambda i: (0, 0)),
        ],
        out_specs=[
            pl.BlockSpec((TM, D_CONV), lambda i: (i, 0)),
            pl.BlockSpec((TM, D_POOL), lambda i: (i, 0)),
        ],
        out_shape=[
            jax.ShapeDtypeStruct((rows, D_CONV), F32),
            jax.ShapeDtypeStruct((rows, D_POOL), F32),
        ],
        compiler_params=pltpu.CompilerParams(
            dimension_semantics=("arbitrary",), vmem_limit_bytes=VMEM_LIMIT),
        name="in_proj",
    )(x, g, w, b)


def _mix_prompt_kernel(n_jobs, u_ref, z_ref, cw_ref, cb_ref, lng_ref, lnb_ref, pw_ref, ps_ref, *rest):
    cast_src, rest = rest[:n_jobs], rest[n_jobs:]
    mix_ref, sc_ref, sp_ref = rest[:3]
    cast_dst, rest = rest[3:3 + n_jobs], rest[3 + n_jobs:]
    uext2, zext2, conv_sc, d_sc = rest
    i = pl.program_id(1)

    _run_casts(cast_src, cast_dst)

    @pl.when(i == 0)
    def _():
        zero = jnp.zeros((SUBLANES, LANES), F32)
        for L in range(N_LANE_TILES):
            for r in range(0, CONV_STG_H, SUBLANES):
                uext2[L, _rows2(r), :] = zero
            for r in range(0, POOL_STG_H, SUBLANES):
                zext2[L, _rows2(r), :] = zero

    @pl.when(i > 0)
    def _():
        for L in range(N_LANE_TILES):
            for r in range(0, CONV_STG_H, SUBLANES):
                uext2[L, _rows2(r), :] = uext2[L, _rows2(TT + r), :]
            for r in range(0, POOL_STG_H, SUBLANES):
                zext2[L, _rows2(r), :] = zext2[L, _rows2(TT + r), :]

    def stage(c, carry):
        r0 = pl.multiple_of(c * SUBLANES, SUBLANES)
        for L in range(N_LANE_TILES):
            uext2[L, _rows2(r0 + CONV_STG_H), :] = u_ref[pl.ds(r0, SUBLANES), _lane(L)]
            zext2[L, _rows2(r0 + POOL_STG_H), :] = z_ref[pl.ds(r0, SUBLANES), _lane(L)]
        return carry

    lax.fori_loop(0, TT // SUBLANES, stage, 0)

    step_rows = SUBLANES * RT
    n_steps = TT // step_rows

    for L in range(N_LANE_TILES):
        def conv_body(c, carry, L=L):
            r0 = pl.multiple_of(c * step_rows, step_rows)
            bias = cb_ref[0:1, _lane(L)]
            accs = [jnp.broadcast_to(bias, (SUBLANES, LANES)) for _ in range(RT)]
            for k in range(CONV_W):
                wk = cw_ref[k:k + 1, _lane(L)]
                for j in range(RT):
                    r = r0 + SUBLANES * j + k + (CONV_STG_H - CONV_H)
                    accs[j] = accs[j] + wk * uext2[L, _rows2(r), :]
            for j in range(RT):
                conv_sc[pl.ds(r0 + SUBLANES * j, SUBLANES), _lane(L)] = accs[j]
            return carry

        lax.fori_loop(0, n_steps, conv_body, 0)

    def window_sum(L, win, rt):
        tok = zext2[L, _rows2(rt + POOL_STG_H), :]
        s = tok
        for m in range(1, win):
            s = s + zext2[L, _rows2(rt + POOL_STG_H - m), :]
        return s, tok

    tiles_per_group = N_LANE_TILES // N_POOL_GROUPS
    for L in range(N_LANE_TILES):
        win = POOL_WINDOWS[L // tiles_per_group]

        def pool_body(c, carry, L=L, win=win):
            r0 = pl.multiple_of(c * step_rows, step_rows)
            for j in range(RT):
                rt = r0 + SUBLANES * j
                s, tok = window_sum(L, win, rt)
                d_sc[pl.ds(rt, SUBLANES), _lane(L)] = s / float(win) - tok
            return carry

        lax.fori_loop(0, n_steps, pool_body, 0)

    @pl.when(i == 0)
    def _():
        for L in range(N_LANE_TILES):
            win = POOL_WINDOWS[L // tiles_per_group]
            for rt in range(0, POOL_STG_H, SUBLANES):
                s, tok = window_sum(L, win, rt)
                pos = rt + lax.broadcasted_iota(jnp.int32, (SUBLANES, LANES), 0)
                cnt = jnp.minimum(win, pos + 1).astype(F32)
                d_sc[rt:rt + SUBLANES, _lane(L)] = s / cnt - tok

    half = TT // 2
    for rc in range(0, TT, half):
        rows = slice(rc, rc + half)
        c = _ln_swish(conv_sc[rows, :], lng_ref[...], lnb_ref[...])
        mix_ref[rows, 0:D_CONV] = c.astype(BF16)
        d = d_sc[rows, :]
        for g in range(N_POOL_GROUPS):
            cols = slice(g * POOL_GROUP_W, (g + 1) * POOL_GROUP_W)
            p = _group_map(d[:, cols], g, pw_ref, ps_ref)
            mix_ref[rows, D_CONV + g * POOL_GROUP_W:D_CONV + (g + 1) * POOL_GROUP_W] = p.astype(BF16)

    @pl.when(i == pl.num_programs(1) - 1)
    def _():
        sc_ref[...] = u_ref[TT - CONV_H:TT, :]
        sp_ref[...] = z_ref[TT - POOL_H:TT, :]


def _mix_prompt(u, z, cw, cb, lng, lnb, pw, ps, cast_jobs=()):
    n_t = SEQ // TT
    row_map = lambda b, i: (b * n_t + i, 0)
    const2 = lambda b, i: (0, 0)
    c_in, c_out, c_shape = _cast_specs(cast_jobs, BATCH * n_t, lambda b, i: b * n_t + i)
    return pl.pallas_call(
        functools.partial(_mix_prompt_kernel, len(cast_jobs)),
        grid=(BATCH, n_t),
        in_specs=[
            pl.BlockSpec((TT, D_CONV), row_map),
            pl.BlockSpec((TT, D_POOL), row_map),
            pl.BlockSpec((CONV_W, D_CONV), const2),
            pl.BlockSpec((1, D_CONV), const2),
            pl.BlockSpec((1, D_CONV), const2),
            pl.BlockSpec((1, D_CONV), const2),
            pl.BlockSpec((N_POOL_GROUPS, POOL_GROUP_W, POOL_GROUP_W), lambda b, i: (0, 0, 0)),
            pl.BlockSpec((1, D_POOL), const2),
        ] + c_in,
        out_specs=[
            pl.BlockSpec((TT, D_MODEL), row_map),
            pl.BlockSpec((None, CONV_H, D_CONV), lambda b, i: (b, 0, 0)),
            pl.BlockSpec((None, POOL_H, D_POOL), lambda b, i: (b, 0, 0)),
        ] + c_out,
        out_shape=[
            jax.ShapeDtypeStruct((BATCH * SEQ, D_MODEL), BF16),
            jax.ShapeDtypeStruct((BATCH, CONV_H, D_CONV), F32),
            jax.ShapeDtypeStruct((BATCH, POOL_H, D_POOL), F32),
        ] + c_shape,
        scratch_shapes=[
            pltpu.VMEM((N_LANE_TILES, 2 * (TT + CONV_STG_H), LANES), F32),
            pltpu.VMEM((N_LANE_TILES, 2 * (TT + POOL_STG_H), LANES), F32),
            pltpu.VMEM((TT, D_CONV), F32),
            pltpu.VMEM((TT, D_POOL), F32),
        ],
        compiler_params=pltpu.CompilerParams(
            dimension_semantics=("arbitrary", "arbitrary"), vmem_limit_bytes=VMEM_LIMIT),
        name="mix_prompt",
    )(u, z, cw, cb, lng, lnb, pw, ps, *[w for w, _ in cast_jobs])


def _mix_sample_kernel(layer, u_ref, z_ref, hc_ref, hp_ref, cw_ref, cb_ref, lng_ref, lnb_ref, pw_ref, ps_ref,
                       *rest):
    if layer > 0:
        rest = rest[2:]
    mix_ref, nc_ref, np_ref = rest

    def uext(j):
        return hc_ref[j] if j < CONV_H else u_ref[j - CONV_H]

    def zext(j):
        return hp_ref[j] if j < POOL_H else z_ref[j - POOL_H]

    for t in range(DEC_SEQ):
        acc = jnp.broadcast_to(cb_ref[...], (SB, D_CONV))
        for k in range(CONV_W):
            acc = acc + cw_ref[k:k + 1, :] * uext(t + k)
        mix_ref[t, :, 0:D_CONV] = _ln_swish(acc, lng_ref[...], lnb_ref[...]).astype(BF16)

        for g in range(N_POOL_GROUPS):
            win = POOL_WINDOWS[g]
            cols = slice(g * POOL_GROUP_W, (g + 1) * POOL_GROUP_W)
            tok = zext(POOL_H + t)[:, cols]
            s = tok
            for m in range(1, win):
                s = s + zext(POOL_H + t - m)[:, cols]
            cnt = float(min(win, PAST_LEN + t + 1))
            p = _group_map(s / cnt - tok, g, pw_ref, ps_ref)
            mix_ref[t, :, D_CONV + g * POOL_GROUP_W:D_CONV + (g + 1) * POOL_GROUP_W] = p.astype(BF16)

    for j in range(CONV_H):
        nc_ref[j] = uext(j + DEC_SEQ)
    for j in range(POOL_H):
        np_ref[j] = zext(j + DEC_SEQ)


def _mix_sample(layer, u, z, hist_c, hist_p, cw, cb, lng, lnb, pw, ps, prev_states):
    tok_map = lambda s: (0, s, 0)
    st_map = lambda s: (layer, 0, s, 0)
    const2 = lambda s: (0, 0)
    n_in = 10
    return pl.pallas_call(
        functools.partial(_mix_sample_kernel, layer),
        grid=(DEC_BATCH // SB,),
        in_specs=[
            pl.BlockSpec((DEC_SEQ, SB, D_CONV), tok_map),
            pl.BlockSpec((DEC_SEQ, SB, D_POOL), tok_map),
            pl.BlockSpec((None, CONV_H, SB, D_CONV), st_map),
            pl.BlockSpec((None, POOL_H, SB, D_POOL), st_map),
            pl.BlockSpec((CONV_W, D_CONV), const2),
            pl.BlockSpec((1, D_CONV), const2),
            pl.BlockSpec((1, D_CONV), const2),
            pl.BlockSpec((1, D_CONV), const2),
            pl.BlockSpec((N_POOL_GROUPS, POOL_GROUP_W, POOL_GROUP_W), lambda s: (0, 0, 0)),
            pl.BlockSpec((1, D_POOL), const2),
        ] + [pl.BlockSpec(memory_space=pl.ANY)] * len(prev_states),
        out_specs=[
            pl.BlockSpec((DEC_SEQ, SB, D_MODEL), tok_map),
            pl.BlockSpec((None, CONV_H, SB, D_CONV), st_map),
            pl.BlockSpec((None, POOL_H, SB, D_POOL), st_map),
        ],
        out_shape=[
            jax.ShapeDtypeStruct((DEC_SEQ, DEC_BATCH, D_MODEL), BF16),
            jax.ShapeDtypeStruct((DEPTH, CONV_H, DEC_BATCH, D_CONV), F32),
            jax.ShapeDtypeStruct((DEPTH, POOL_H, DEC_BATCH, D_POOL), F32),
        ],
        input_output_aliases={n_in + k: 1 + k for k in range(len(prev_states))},
        compiler_params=pltpu.CompilerParams(
            dimension_semantics=("arbitrary",), vmem_limit_bytes=VMEM_LIMIT),
        name="mix_sample",
    )(u, z, hist_c, hist_p, cw, cb, lng, lnb, pw, ps, *prev_states)


def _out_ffn_kernel(n_jobs, final_norm, x_ref, mix_ref, wo_ref, g2_ref, w1_ref, w2_ref, gf_ref, *rest):
    cast_src, rest = rest[:n_jobs], rest[n_jobs:]
    o_ref = rest[0]
    cast_dst, (h2_ref,) = rest[1:1 + n_jobs], rest[1 + n_jobs:]
    j = pl.program_id(1)

    _run_casts(cast_src, cast_dst)

    @pl.when(j == 0)
    def _():
        x1 = x_ref[...] + jnp.dot(mix_ref[...], wo_ref[...], preferred_element_type=F32)
        o_ref[...] = x1
        h2_ref[...] = _rms(x1, g2_ref[...]).astype(BF16)

    f = jnp.dot(h2_ref[...], w1_ref[...], preferred_element_type=F32)
    f = jnp.square(jnp.maximum(f, 0.0)).astype(BF16)
    o_ref[...] += jnp.dot(f, w2_ref[...], preferred_element_type=F32)

    if final_norm:
        @pl.when(j == pl.num_programs(1) - 1)
        def _():
            o_ref[...] = _rms(o_ref[...], gf_ref[...])


def _out_ffn(x, mix, wo, g2, w1, w2, gf, final_norm, cast_jobs=()):
    rows = x.shape[0]
    n_f = D_FF // TF
    c_in, c_out, c_shape = _cast_specs(cast_jobs, (rows // TM) * n_f, lambda i, j: i * n_f + j)
    return pl.pallas_call(
        functools.partial(_out_ffn_kernel, len(cast_jobs), final_norm),
        grid=(rows // TM, n_f),
        in_specs=[
            pl.BlockSpec((TM, D_MODEL), lambda i, j: (i, 0)),
            pl.BlockSpec((TM, D_MODEL), lambda i, j: (i, 0)),
            pl.BlockSpec((D_MODEL, D_MODEL), lambda i, j: (0, 0), pipeline_mode=pl.Buffered(1)),
            pl.BlockSpec((1, D_MODEL), lambda i, j: (0, 0)),
            pl.BlockSpec((D_MODEL, TF), lambda i, j: (0, j)),
            pl.BlockSpec((TF, D_MODEL), lambda i, j: (j, 0)),
            pl.BlockSpec((1, D_MODEL), lambda i, j: (0, 0)),
        ] + c_in,
        out_specs=[pl.BlockSpec((TM, D_MODEL), lambda i, j: (i, 0))] + c_out,
        out_shape=[jax.ShapeDtypeStruct((rows, D_MODEL), F32)] + c_shape,
        scratch_shapes=[pltpu.VMEM((TM, D_MODEL), BF16)],
        compiler_params=pltpu.CompilerParams(
            dimension_semantics=("arbitrary", "arbitrary"), vmem_limit_bytes=VMEM_LIMIT),
        name="out_ffn",
    )(x, mix, wo, g2, w1, w2, gf, *[w for w, _ in cast_jobs])


def kernel(x_prompt, x_sample, state_conv, state_pool, norm1_g, w_in, b_in, conv_w, conv_b, ln_g, ln_b, pool_w, pool_scale, w_out, norm2_g, w_ff1, w_ff2, norm_f):
    xp = x_prompt.reshape(BATCH * SEQ, D_MODEL)
    xs = x_sample.transpose(1, 0, 2).reshape(DEC_SEQ * DEC_BATCH, D_MODEL)
    hist_c = state_conv.transpose(0, 2, 1, 3)
    hist_p = state_pool.transpose(0, 2, 1, 3)
    pool_w_b = pool_w.astype(BF16)
    w_in_b, w_out_b = w_in[0].astype(BF16), w_out[0].astype(BF16)
    w_ff1_b = w_ff2_b = None

    conv_p, pool_p = [], []
    sample_states = ()
    for l in range(DEPTH):
        row = lambda a: a[l][None, :]
        mixer_w = (conv_w[l], row(conv_b), row(ln_g), row(ln_b), pool_w_b[l], row(pool_scale))
        final = l == DEPTH - 1
        as3 = lambda a: a.reshape(DEC_SEQ, DEC_BATCH, a.shape[-1])

        up, zp = _in_proj(xp, row(norm1_g), w_in_b, row(b_in))
        us, zs = _in_proj(xs, row(norm1_g), w_in_b, row(b_in))
        jobs = ((w_ff1, 0), (w_ff2, 0)) if l == 0 else ()
        mix_p, cp, pp, *cast = _mix_prompt(up, zp, *mixer_w, cast_jobs=jobs)
        if l == 0:
            w_ff1_b, w_ff2_b = cast
        mix_s, *sample_states = _mix_sample(l, as3(us), as3(zs), hist_c, hist_p, *mixer_w, sample_states)
        ffn_w = (w_out_b, row(norm2_g), w_ff1_b, w_ff2_b, norm_f[None, :])
        jobs = () if final else ((w_in, l + 1), (w_out, l + 1), (w_ff1, l + 1), (w_ff2, l + 1))
        xs, = _out_ffn(xs, mix_s.reshape(DEC_SEQ * DEC_BATCH, D_MODEL), *ffn_w, final)
        xp, *cast = _out_ffn(xp, mix_p, *ffn_w, final, cast_jobs=jobs)
        if not final:
            w_in_b, w_out_b, w_ff1_b, w_ff2_b = cast
        conv_p.append(cp)
        pool_p.append(pp)

    conv_s, pool_s = sample_states
    y_sample = xs.reshape(DEC_SEQ, DEC_BATCH, D_MODEL).transpose(1, 0, 2)
    return (xp.reshape(BATCH, SEQ, D_MODEL), y_sample, jnp.stack(conv_p), jnp.stack(pool_p),
            conv_s.transpose(0, 2, 1, 3), pool_s.transpose(0, 2, 1, 3))
```

```python
import functools

import jax
import jax.numpy as jnp
from jax import lax
from jax.experimental import pallas as pl
from jax.experimental.pallas import tpu as pltpu

F32 = jnp.float32
BF16 = jnp.bfloat16

D_MODEL = 2048
BATCH = 4
SEQ = 2048
DEPTH = 2
DEC_BATCH = 128
DEC_SEQ = 4
PAST_LEN = 16384
D_CONV = 1024
D_POOL = 1024
N_POOL_GROUPS = 4
POOL_GROUP_W = D_POOL // N_POOL_GROUPS
POOL_WINDOWS = (2, 4, 8, 16)
POOL_MAX = 16
CONV_W = 31
D_FF = 4 * D_MODEL
D_IN = 2 * D_CONV + D_POOL
EPS = 1e-6

CONV_H = CONV_W - 1
POOL_H = POOL_MAX - 1

LANES = 128
SUBLANES = 8
BF16_ROWS = 16
MXU_COLS = 256
N_LANE_TILES = D_CONV // LANES
TILES_PER_GROUP = N_LANE_TILES // N_POOL_GROUPS
VMEM_LIMIT = 60 * 1024 * 1024

TM = 512
TF = 1024
N_F = D_FF // TF
CH = TM // N_F
CONV_STG_H = 32
POOL_STG_H = 16
SB = 32
assert SEQ % TM == 0 and CH % BF16_ROWS == 0 and CH >= CONV_STG_H


def _lane(L):
    return slice(L * LANES, (L + 1) * LANES)


def _rows2(r, n=SUBLANES):
    return pl.ds(2 * r, n, stride=2)


def _rms(x, g):
    return x * lax.rsqrt(jnp.mean(x * x, axis=-1, keepdims=True) + EPS) * g


def _ln_swish(x, g, b):
    mu = jnp.mean(x, axis=-1, keepdims=True)
    xc = x - mu
    var = jnp.mean(xc * xc, axis=-1, keepdims=True)
    y = xc * lax.rsqrt(var + EPS) * g + b
    return y * jax.nn.sigmoid(y)


def _group_map(d, g, pw_ref, ps_ref):
    cols = slice(g * POOL_GROUP_W, (g + 1) * POOL_GROUP_W)
    return jnp.dot(d.astype(BF16), pw_ref[g], preferred_element_type=F32) * ps_ref[:, cols]


def _cast_specs(jobs, n_steps, step_of):
    in_specs, out_specs, out_shapes = [], [], []
    for w, layer in jobs:
        _, r, c = w.shape
        rows = r // n_steps
        assert rows * n_steps == r and rows % BF16_ROWS == 0
        in_specs.append(pl.BlockSpec(
            (None, rows, c), lambda *ids, layer=layer: (layer, step_of(*ids), 0)))
        out_specs.append(pl.BlockSpec((rows, c), lambda *ids: (step_of(*ids), 0)))
        out_shapes.append(jax.ShapeDtypeStruct((r, c), BF16))
    return in_specs, out_specs, out_shapes


def _run_casts(src_refs, dst_refs):
    for src, dst in zip(src_refs, dst_refs):
        dst[...] = src[...].astype(BF16)


def _in_proj_kernel(n_jobs, x_ref, g_ref, w_ref, b_ref, *rest):
    cast_src, rest = rest[:n_jobs], rest[n_jobs:]
    u_ref, z_ref = rest[:2]
    _run_casts(cast_src, rest[2:])
    h = _rms(x_ref[...], g_ref[...]).astype(BF16)

    def proj(lo, hi):
        return jnp.dot(h, w_ref[:, lo:hi], preferred_element_type=F32) + b_ref[:, lo:hi]

    a = proj(0, D_CONV)
    gate = proj(D_CONV, 2 * D_CONV)
    u_ref[...] = a * jax.nn.sigmoid(gate)
    z_ref[...] = proj(2 * D_CONV, D_IN)


def _in_proj(x, g, w, b, cast_jobs=()):
    rows = x.shape[0]
    c_in, c_out, c_shape = _cast_specs(cast_jobs, rows // TM, lambda i: i)
    return pl.pallas_call(
        functools.partial(_in_proj_kernel, len(cast_jobs)),
        grid=(rows // TM,),
        in_specs=[
            pl.BlockSpec((TM, D_MODEL), lambda i: (i, 0)),
            pl.BlockSpec((1, D_MODEL), lambda i: (0, 0)),
            pl.BlockSpec((D_MODEL, D_IN), lambda i: (0, 0), pipeline_mode=pl.Buffered(1)),
            pl.BlockSpec((1, D_IN), lambda i: (0, 0)),
        ] + c_in,
        out_specs=[
            pl.BlockSpec((TM, D_CONV), lambda i: (i, 0)),
            pl.BlockSpec((TM, D_POOL), lambda i: (i, 0)),
        ] + c_out,
        out_shape=[
            jax.ShapeDtypeStruct((rows, D_CONV), F32),
            jax.ShapeDtypeStruct((rows, D_POOL), F32),
        ] + c_shape,
        compiler_params=pltpu.CompilerParams(
            dimension_semantics=("arbitrary",), vmem_limit_bytes=VMEM_LIMIT),
        name="in_proj",
    )(x, g, w, b, *[w_ for w_, _ in cast_jobs])


def _mixer_chunk(tile, j, u_ref, z_ref, cw_ref, cb_ref, lng_ref, lnb_ref,
                 mix_sc, sc_ref, sp_ref, uext2, zext2, conv_sc, d_sc):
    tile_in_seq = tile % (SEQ // TM)
    seq_start = jnp.logical_and(tile_in_seq == 0, j == 0)
    zero = jnp.zeros((SUBLANES, LANES), F32)
    n_rt = CH // SUBLANES

    for L in range(N_LANE_TILES):
        for r in range(0, CONV_STG_H, SUBLANES):
            uext2[L, _rows2(r), :] = jnp.where(seq_start, zero, uext2[L, _rows2(CH + r), :])
        for r in range(0, POOL_STG_H, SUBLANES):
            zext2[L, _rows2(r), :] = jnp.where(seq_start, zero, zext2[L, _rows2(CH + r), :])
    for L in range(N_LANE_TILES):
        for rt in range(n_rt):
            r0 = rt * SUBLANES
            uext2[L, _rows2(r0 + CONV_STG_H), :] = u_ref[r0:r0 + SUBLANES, _lane(L)]
            zext2[L, _rows2(r0 + POOL_STG_H), :] = z_ref[r0:r0 + SUBLANES, _lane(L)]

    conv_done = []
    for L in range(N_LANE_TILES):
        bias = jnp.broadcast_to(cb_ref[0:1, _lane(L)], (SUBLANES, LANES))
        accs = [bias] * n_rt
        for k in range(CONV_W):
            wk = cw_ref[k:k + 1, _lane(L)]
            for rt in range(n_rt):
                r = rt * SUBLANES + k + (CONV_STG_H - CONV_H)
                accs[rt] = accs[rt] + wk * uext2[L, _rows2(r), :]
        for rt in range(n_rt):
            conv_sc[rt * SUBLANES:(rt + 1) * SUBLANES, _lane(L)] = accs[rt]
        conv_done.append(accs)

    pos0 = tile_in_seq * TM + j * CH
    row = lax.broadcasted_iota(jnp.int32, (SUBLANES, LANES), 0)
    inv_cnt = {}
    for win in POOL_WINDOWS:
        for rt in range(n_rt):
            cnt = jnp.minimum(win, pos0 + rt * SUBLANES + row + 1).astype(F32)
            inv_cnt[win, rt] = 1.0 / cnt
    pool_done = []
    for L in range(N_LANE_TILES):
        win = POOL_WINDOWS[L // TILES_PER_GROUP]
        ds = []
        for rt in range(n_rt):
            base = rt * SUBLANES + POOL_STG_H
            tok = zext2[L, _rows2(base), :]
            s = tok
            for m in range(1, win):
                s = s + zext2[L, _rows2(base - m), :]
            d = s * inv_cnt[win, rt] - tok
            d_sc[rt * SUBLANES:(rt + 1) * SUBLANES, _lane(L)] = d
            ds.append(d)
        pool_done.append(ds)

    rows = pl.ds(pl.multiple_of(j * CH, CH), CH)
    c = _ln_swish(conv_sc[...], lng_ref[...], lnb_ref[...])
    mix_sc[rows, 0:D_CONV] = c.astype(BF16)
    mix_sc[rows, D_CONV:D_MODEL] = d_sc[...].astype(BF16)

    sc_ref[...] = u_ref[CH - CONV_H:CH, :]
    sp_ref[...] = z_ref[CH - POOL_H:CH, :]
    ln_done = [c[r:r + SUBLANES, _lane(L)] for r in range(0, CH, SUBLANES) for L in range(N_LANE_TILES)]
    return conv_done, pool_done, ln_done


def _tied(x, deps, zero_ref):
    if not deps:
        return x
    bits = lax.bitcast_convert_type(deps[0], jnp.int32)
    for d in deps[1:]:
        bits = bits | lax.bitcast_convert_type(d, jnp.int32)
    z = lax.bitcast_convert_type(bits & zero_ref[...], F32)[0:1, :]
    return x + jnp.concatenate([z] * (x.shape[1] // LANES), axis=1)


def _prompt_mix_dot(mix_sc, pw_ref, ps_ref, wo_ref):
    for g in range(N_POOL_GROUPS):
        cols = slice(D_CONV + g * POOL_GROUP_W, D_CONV + (g + 1) * POOL_GROUP_W)
        mix_sc[:, cols] = _group_map(mix_sc[:, cols], g, pw_ref, ps_ref).astype(BF16)
    return jnp.dot(mix_sc[...], wo_ref[...], preferred_element_type=F32)


def _ffn_first(j, x_ref, mix_dot, g2_ref, o_ref, h2_ref):
    @pl.when(j == 0)
    def _():
        x1 = x_ref[...] + mix_dot()
        o_ref[...] = x1
        h2_ref[...] = _rms(x1, g2_ref[...]).astype(BF16)


def _ffn_tile(w1_ref, w2_ref, o_ref, h2_ref):
    f = jnp.dot(h2_ref[...], w1_ref[...], preferred_element_type=F32)
    f = jnp.square(jnp.maximum(f, 0.0)).astype(BF16)
    o_ref[...] += jnp.dot(f, w2_ref[...], preferred_element_type=F32)


def _ffn_tile_tied(w1_ref, w2_ref, o_ref, h2_ref, f_sc, after):
    n1 = TF // MXU_COLS
    for n in range(n1):
        cols = slice(n * MXU_COLS, (n + 1) * MXU_COLS)
        r = after(n, jnp.dot(h2_ref[...], w1_ref[:, cols], preferred_element_type=F32))
        f_sc[:, cols] = jnp.square(jnp.maximum(r, 0.0)).astype(BF16)
    for n in range(D_MODEL // MXU_COLS):
        cols = slice(n * MXU_COLS, (n + 1) * MXU_COLS)
        acc = o_ref[:, cols] + jnp.dot(f_sc[...], w2_ref[:, cols], preferred_element_type=F32)
        o_ref[:, cols] = after(n1 + n, acc)


def _final_norm(j, o_ref, gf_ref):
    @pl.when(j == N_F - 1)
    def _():
        o_ref[...] = _rms(o_ref[...], gf_ref[...])


def _mix_ffn_kernel(n_jobs, final_norm, n_tiles,
                    u_ref, z_ref, cw_ref, cb_ref, lng_ref, lnb_ref, pw_ref, ps_ref,
                    x_ref, wo_ref, g2_ref, w1_ref, w2_ref, gf_ref, zero_ref, *rest):
    cast_src, rest = rest[:n_jobs], rest[n_jobs:]
    o_ref, sc_ref, sp_ref = rest[:3]
    cast_dst, rest = rest[3:3 + n_jobs], rest[3 + n_jobs:]
    h2_ref, mix_sc, f_sc, uext2, zext2, conv_sc, d_sc = rest
    i = pl.program_id(0)
    j = pl.program_id(1)
    tile = jnp.minimum(i, n_tiles - 1)

    _run_casts(cast_src, cast_dst)

    def mixer():
        return _mixer_chunk(tile, j, u_ref, z_ref, cw_ref, cb_ref, lng_ref, lnb_ref,
                            mix_sc, sc_ref, sp_ref, uext2, zext2, conv_sc, d_sc)

    @pl.when(i == 0)
    def _():
        @pl.when(j == 0)
        def _():
            zero = jnp.zeros((SUBLANES, LANES), F32)
            for L in range(N_LANE_TILES):
                for r in range(0, CONV_STG_H, SUBLANES):
                    uext2[L, _rows2(CH + r), :] = zero
                for r in range(0, POOL_STG_H, SUBLANES):
                    zext2[L, _rows2(CH + r), :] = zero
        mixer()

    @pl.when(i > 0)
    def _():
        mix_dot = functools.partial(_prompt_mix_dot, mix_sc, pw_ref, ps_ref, wo_ref)
        _ffn_first(j, x_ref, mix_dot, g2_ref, o_ref, h2_ref)
        conv_done, pool_done, ln_done = mixer()

        pieces = {b: [conv_done[b]] for b in range(N_LANE_TILES)}
        pieces[8] = pool_done[:N_LANE_TILES // 2]
        pieces[9] = pool_done[N_LANE_TILES // 2:]
        pieces[10] = [ln_done]

        def after(b, x):
            return _tied(x, [v for piece in pieces.get(b, ()) for v in piece], zero_ref)

        _ffn_tile_tied(w1_ref, w2_ref, o_ref, h2_ref, f_sc, after)
        if final_norm:
            _final_norm(j, o_ref, gf_ref)


def _mix_ffn(u, z, cw, cb, lng, lnb, pw, ps, x, wo, g2, w1, w2, gf, final_norm, cast_jobs=()):
    rows = x.shape[0]
    n_tiles = rows // TM
    tiles_per_seq = SEQ // TM
    mix_tile = lambda i: jnp.minimum(i, n_tiles - 1)
    ffn_tile = lambda i: jnp.maximum(i - 1, 0)
    ffn_j = lambda i, j: jnp.where(i > 0, j, 0)
    const2 = lambda i, j: (0, 0)
    c_in, c_out, c_shape = _cast_specs(
        cast_jobs, n_tiles * N_F, lambda i, j: ffn_tile(i) * N_F + ffn_j(i, j))
    return pl.pallas_call(
        functools.partial(_mix_ffn_kernel, len(cast_jobs), final_norm, n_tiles),
        grid=(n_tiles + 1, N_F),
        in_specs=[
            pl.BlockSpec((CH, D_CONV), lambda i, j: (mix_tile(i) * N_F + j, 0)),
            pl.BlockSpec((CH, D_POOL), lambda i, j: (mix_tile(i) * N_F + j, 0)),
            pl.BlockSpec((CONV_W, D_CONV), const2),
            pl.BlockSpec((1, D_CONV), const2),
            pl.BlockSpec((1, D_CONV), const2),
            pl.BlockSpec((1, D_CONV), const2),
            pl.BlockSpec((N_POOL_GROUPS, POOL_GROUP_W, POOL_GROUP_W), lambda i, j: (0, 0, 0)),
            pl.BlockSpec((1, D_POOL), const2),
            pl.BlockSpec((TM, D_MODEL), lambda i, j: (ffn_tile(i), 0)),
            pl.BlockSpec((D_MODEL, D_MODEL), const2, pipeline_mode=pl.Buffered(1)),
            pl.BlockSpec((1, D_MODEL), const2),
            pl.BlockSpec((D_MODEL, TF), lambda i, j: (0, ffn_j(i, j))),
            pl.BlockSpec((TF, D_MODEL), lambda i, j: (ffn_j(i, j), 0)),
            pl.BlockSpec((1, D_MODEL), const2),
            pl.BlockSpec((SUBLANES, LANES), const2),
        ] + c_in,
        out_specs=[
            pl.BlockSpec((TM, D_MODEL), lambda i, j: (ffn_tile(i), 0)),
            pl.BlockSpec((None, CONV_H, D_CONV), lambda i, j: (mix_tile(i) // tiles_per_seq, 0, 0)),
            pl.BlockSpec((None, POOL_H, D_POOL), lambda i, j: (mix_tile(i) // tiles_per_seq, 0, 0)),
        ] + c_out,
        out_shape=[
            jax.ShapeDtypeStruct((rows, D_MODEL), F32),
            jax.ShapeDtypeStruct((BATCH, CONV_H, D_CONV), F32),
            jax.ShapeDtypeStruct((BATCH, POOL_H, D_POOL), F32),
        ] + c_shape,
        scratch_shapes=[
            pltpu.VMEM((TM, D_MODEL), BF16),
            pltpu.VMEM((TM, D_MODEL), BF16),
            pltpu.VMEM((TM, TF), BF16),
            pltpu.VMEM((N_LANE_TILES, 2 * (CH + CONV_STG_H), LANES), F32),
            pltpu.VMEM((N_LANE_TILES, 2 * (CH + POOL_STG_H), LANES), F32),
            pltpu.VMEM((CH, D_CONV), F32),
            pltpu.VMEM((CH, D_POOL), F32),
        ],
        compiler_params=pltpu.CompilerParams(
            dimension_semantics=("arbitrary", "arbitrary"), vmem_limit_bytes=VMEM_LIMIT),
        name="mix_ffn",
    )(u, z, cw, cb, lng, lnb, pw, ps, x, wo, g2, w1, w2, gf, jnp.zeros((SUBLANES, LANES), jnp.int32),
      *[w_ for w_, _ in cast_jobs])


def _mix_sample_kernel(layer, u_ref, z_ref, hc_ref, hp_ref, cw_ref, cb_ref, lng_ref, lnb_ref, pw_ref, ps_ref,
                       *rest):
    if layer > 0:
        rest = rest[2:]
    mix_ref, nc_ref, np_ref = rest

    def uext(j):
        return hc_ref[j] if j < CONV_H else u_ref[j - CONV_H]

    def zext(j):
        return hp_ref[j] if j < POOL_H else z_ref[j - POOL_H]

    for t in range(DEC_SEQ):
        acc = jnp.broadcast_to(cb_ref[...], (SB, D_CONV))
        for k in range(CONV_W):
            acc = acc + cw_ref[k:k + 1, :] * uext(t + k)
        mix_ref[t, :, 0:D_CONV] = _ln_swish(acc, lng_ref[...], lnb_ref[...]).astype(BF16)

        for g in range(N_POOL_GROUPS):
            win = POOL_WINDOWS[g]
            cols = slice(g * POOL_GROUP_W, (g + 1) * POOL_GROUP_W)
            tok = zext(POOL_H + t)[:, cols]
            s = tok
            for m in range(1, win):
                s = s + zext(POOL_H + t - m)[:, cols]
            cnt = float(min(win, PAST_LEN + t + 1))
            p = _group_map(s / cnt - tok, g, pw_ref, ps_ref)
            mix_ref[t, :, D_CONV + g * POOL_GROUP_W:D_CONV + (g + 1) * POOL_GROUP_W] = p.astype(BF16)

    for j in range(CONV_H):
        nc_ref[j] = uext(j + DEC_SEQ)
    for j in range(POOL_H):
        np_ref[j] = zext(j + DEC_SEQ)


def _mix_sample(layer, u, z, hist_c, hist_p, cw, cb, lng, lnb, pw, ps, prev_states):
    tok_map = lambda s: (0, s, 0)
    st_map = lambda s: (layer, 0, s, 0)
    const2 = lambda s: (0, 0)
    n_in = 10
    return pl.pallas_call(
        functools.partial(_mix_sample_kernel, layer),
        grid=(DEC_BATCH // SB,),
        in_specs=[
            pl.BlockSpec((DEC_SEQ, SB, D_CONV), tok_map),
            pl.BlockSpec((DEC_SEQ, SB, D_POOL), tok_map),
            pl.BlockSpec((None, CONV_H, SB, D_CONV), st_map),
            pl.BlockSpec((None, POOL_H, SB, D_POOL), st_map),
            pl.BlockSpec((CONV_W, D_CONV), const2),
            pl.BlockSpec((1, D_CONV), const2),
            pl.BlockSpec((1, D_CONV), const2),
            pl.BlockSpec((1, D_CONV), const2),
            pl.BlockSpec((N_POOL_GROUPS, POOL_GROUP_W, POOL_GROUP_W), lambda s: (0, 0, 0)),
            pl.BlockSpec((1, D_POOL), const2),
        ] + [pl.BlockSpec(memory_space=pl.ANY)] * len(prev_states),
        out_specs=[
            pl.BlockSpec((DEC_SEQ, SB, D_MODEL), tok_map),
            pl.BlockSpec((None, CONV_H, SB, D_CONV), st_map),
            pl.BlockSpec((None, POOL_H, SB, D_POOL), st_map),
        ],
        out_shape=[
            jax.ShapeDtypeStruct((DEC_SEQ, DEC_BATCH, D_MODEL), BF16),
            jax.ShapeDtypeStruct((DEPTH, CONV_H, DEC_BATCH, D_CONV), F32),
            jax.ShapeDtypeStruct((DEPTH, POOL_H, DEC_BATCH, D_POOL), F32),
        ],
        input_output_aliases={n_in + k: 1 + k for k in range(len(prev_states))},
        compiler_params=pltpu.CompilerParams(
            dimension_semantics=("arbitrary",), vmem_limit_bytes=VMEM_LIMIT),
        name="mix_sample",
    )(u, z, hist_c, hist_p, cw, cb, lng, lnb, pw, ps, *prev_states)


def _out_ffn_kernel(final_norm, x_ref, mix_ref, wo_ref, g2_ref, w1_ref, w2_ref, gf_ref, o_ref, h2_ref):
    j = pl.program_id(1)
    mix_dot = lambda: jnp.dot(mix_ref[...], wo_ref[...], preferred_element_type=F32)
    _ffn_first(j, x_ref, mix_dot, g2_ref, o_ref, h2_ref)
    _ffn_tile(w1_ref, w2_ref, o_ref, h2_ref)
    if final_norm:
        _final_norm(j, o_ref, gf_ref)


def _out_ffn(x, mix, wo, g2, w1, w2, gf, final_norm):
    rows = x.shape[0]
    return pl.pallas_call(
        functools.partial(_out_ffn_kernel, final_norm),
        grid=(rows // TM, N_F),
        in_specs=[
            pl.BlockSpec((TM, D_MODEL), lambda i, j: (i, 0)),
            pl.BlockSpec((TM, D_MODEL), lambda i, j: (i, 0)),
            pl.BlockSpec((D_MODEL, D_MODEL), lambda i, j: (0, 0), pipeline_mode=pl.Buffered(1)),
            pl.BlockSpec((1, D_MODEL), lambda i, j: (0, 0)),
            pl.BlockSpec((D_MODEL, TF), lambda i, j: (0, j)),
            pl.BlockSpec((TF, D_MODEL), lambda i, j: (j, 0)),
            pl.BlockSpec((1, D_MODEL), lambda i, j: (0, 0)),
        ],
        out_specs=pl.BlockSpec((TM, D_MODEL), lambda i, j: (i, 0)),
        out_shape=jax.ShapeDtypeStruct((rows, D_MODEL), F32),
        scratch_shapes=[pltpu.VMEM((TM, D_MODEL), BF16)],
        compiler_params=pltpu.CompilerParams(
            dimension_semantics=("arbitrary", "arbitrary"), vmem_limit_bytes=VMEM_LIMIT),
        name="out_ffn",
    )(x, mix, wo, g2, w1, w2, gf)


def kernel(x_prompt, x_sample, state_conv, state_pool, norm1_g, w_in, b_in, conv_w, conv_b, ln_g, ln_b, pool_w, pool_scale, w_out, norm2_g, w_ff1, w_ff2, norm_f):
    xp = x_prompt.reshape(BATCH * SEQ, D_MODEL)
    xs = x_sample.transpose(1, 0, 2).reshape(DEC_SEQ * DEC_BATCH, D_MODEL)
    hist_c = state_conv.transpose(0, 2, 1, 3)
    hist_p = state_pool.transpose(0, 2, 1, 3)
    pool_w_b = pool_w.astype(BF16)
    w_in_b, w_out_b = w_in[0].astype(BF16), w_out[0].astype(BF16)
    w_ff1_b = w_ff2_b = None

    conv_p, pool_p = [], []
    sample_states = ()
    for l in range(DEPTH):
        row = lambda a: a[l][None, :]
        mixer_w = (conv_w[l], row(conv_b), row(ln_g), row(ln_b), pool_w_b[l], row(pool_scale))
        final = l == DEPTH - 1
        as3 = lambda a: a.reshape(DEC_SEQ, DEC_BATCH, a.shape[-1])

        jobs = ((w_ff1, 0), (w_ff2, 0)) if l == 0 else ()
        up, zp, *cast = _in_proj(xp, row(norm1_g), w_in_b, row(b_in), cast_jobs=jobs)
        if l == 0:
            w_ff1_b, w_ff2_b = cast
        us, zs = _in_proj(xs, row(norm1_g), w_in_b, row(b_in))
        mix_s, *sample_states = _mix_sample(l, as3(us), as3(zs), hist_c, hist_p, *mixer_w, sample_states)
        ffn_w = (w_out_b, row(norm2_g), w_ff1_b, w_ff2_b, norm_f[None, :])
        xs = _out_ffn(xs, mix_s.reshape(DEC_SEQ * DEC_BATCH, D_MODEL), *ffn_w, final)
        jobs = () if final else ((w_in, l + 1), (w_out, l + 1), (w_ff1, l + 1), (w_ff2, l + 1))
        xp, cp, pp, *cast = _mix_ffn(up, zp, *mixer_w, xp, *ffn_w, final, cast_jobs=jobs)
        if not final:
            w_in_b, w_out_b, w_ff1_b, w_ff2_b = cast
        conv_p.append(cp)
        pool_p.append(pp)

    conv_s, pool_s = sample_states
    y_sample = xs.reshape(DEC_SEQ, DEC_BATCH, D_MODEL).transpose(1, 0, 2)
    return (xp.reshape(BATCH, SEQ, D_MODEL), y_sample, jnp.stack(conv_p), jnp.stack(pool_p),
            conv_s.transpose(0, 2, 1, 3), pool_s.transpose(0, 2, 1, 3))
```

```python
import functools

import jax
import jax.numpy as jnp
from jax import lax
from jax.experimental import pallas as pl
from jax.experimental.pallas import tpu as pltpu

F32 = jnp.float32
BF16 = jnp.bfloat16

D_MODEL = 2048
BATCH = 4
SEQ = 2048
DEPTH = 2
DEC_BATCH = 128
DEC_SEQ = 4
PAST_LEN = 16384
D_CONV = 1024
D_POOL = 1024
N_POOL_GROUPS = 4
POOL_GROUP_W = D_POOL // N_POOL_GROUPS
POOL_WINDOWS = (2, 4, 8, 16)
POOL_MAX = 16
CONV_W = 31
D_FF = 4 * D_MODEL
D_IN = 2 * D_CONV + D_POOL
EPS = 1e-6

CONV_H = CONV_W - 1
POOL_H = POOL_MAX - 1

LANES = 128
SUBLANES = 8
BF16_ROWS = 16
MXU_COLS = 256
N_LANE_TILES = D_CONV // LANES
TILES_PER_GROUP = N_LANE_TILES // N_POOL_GROUPS
VMEM_LIMIT = 60 * 1024 * 1024

TM = 512
TF = 1024
N_F = D_FF // TF
CH = TM // N_F
CONV_STG_H = 32
POOL_STG_H = 16
SB = 32
assert SEQ % TM == 0 and CH % BF16_ROWS == 0 and CH >= CONV_STG_H


def _lane(L):
    return slice(L * LANES, (L + 1) * LANES)


def _rows2(r, n=SUBLANES):
    return pl.ds(2 * r, n, stride=2)


def _rms(x, g):
    return x * lax.rsqrt(jnp.mean(x * x, axis=-1, keepdims=True) + EPS) * g


def _ln_swish(x, g, b):
    mu = jnp.mean(x, axis=-1, keepdims=True)
    xc = x - mu
    var = jnp.mean(xc * xc, axis=-1, keepdims=True)
    y = xc * lax.rsqrt(var + EPS) * g + b
    return y * jax.nn.sigmoid(y)


def _group_map(d, g, pw_ref, ps_ref):
    cols = slice(g * POOL_GROUP_W, (g + 1) * POOL_GROUP_W)
    return jnp.dot(d.astype(BF16), pw_ref[g], preferred_element_type=F32) * ps_ref[:, cols]


def _cast_specs(jobs, n_steps, step_of):
    in_specs, out_specs, out_shapes = [], [], []
    for w, layer in jobs:
        _, r, c = w.shape
        rows = r // n_steps
        assert rows * n_steps == r and rows % BF16_ROWS == 0
        in_specs.append(pl.BlockSpec(
            (None, rows, c), lambda *ids, layer=layer: (layer, step_of(*ids), 0)))
        out_specs.append(pl.BlockSpec((rows, c), lambda *ids: (step_of(*ids), 0)))
        out_shapes.append(jax.ShapeDtypeStruct((r, c), BF16))
    return in_specs, out_specs, out_shapes


def _run_casts(src_refs, dst_refs):
    for src, dst in zip(src_refs, dst_refs):
        dst[...] = src[...].astype(BF16)


def _in_proj_kernel(n_jobs, x_ref, g_ref, w_ref, b_ref, *rest):
    cast_src, rest = rest[:n_jobs], rest[n_jobs:]
    u_ref, z_ref = rest[:2]
    _run_casts(cast_src, rest[2:])
    h = _rms(x_ref[...], g_ref[...]).astype(BF16)

    def proj(lo, hi):
        return jnp.dot(h, w_ref[:, lo:hi], preferred_element_type=F32) + b_ref[:, lo:hi]

    a = proj(0, D_CONV)
    gate = proj(D_CONV, 2 * D_CONV)
    u_ref[...] = a * jax.nn.sigmoid(gate)
    z_ref[...] = proj(2 * D_CONV, D_IN)


def _in_proj(x, g, w, b, cast_jobs=()):
    rows = x.shape[0]
    c_in, c_out, c_shape = _cast_specs(cast_jobs, rows // TM, lambda i: i)
    return pl.pallas_call(
        functools.partial(_in_proj_kernel, len(cast_jobs)),
        grid=(rows // TM,),
        in_specs=[
            pl.BlockSpec((TM, D_MODEL), lambda i: (i, 0)),
            pl.BlockSpec((1, D_MODEL), lambda i: (0, 0)),
            pl.BlockSpec((D_MODEL, D_IN), lambda i: (0, 0), pipeline_mode=pl.Buffered(1)),
            pl.BlockSpec((1, D_IN), lambda i: (0, 0)),
        ] + c_in,
        out_specs=[
            pl.BlockSpec((TM, D_CONV), lambda i: (i, 0)),
            pl.BlockSpec((TM, D_POOL), lambda i: (i, 0)),
        ] + c_out,
        out_shape=[
            jax.ShapeDtypeStruct((rows, D_CONV), F32),
            jax.ShapeDtypeStruct((rows, D_POOL), F32),
        ] + c_shape,
        compiler_params=pltpu.CompilerParams(
            dimension_semantics=("arbitrary",), vmem_limit_bytes=VMEM_LIMIT),
        name="in_proj",
    )(x, g, w, b, *[w_ for w_, _ in cast_jobs])


def _mixer_chunk(tile, j, u_ref, z_ref, cw_ref, cb_ref, lng_ref, lnb_ref,
                 mix_sc, sc_ref, sp_ref, uext2, zext2, conv_sc, d_sc):
    tile_in_seq = tile % (SEQ // TM)
    seq_start = jnp.logical_and(tile_in_seq == 0, j == 0)
    zero = jnp.zeros((SUBLANES, LANES), F32)
    n_rt = CH // SUBLANES

    for L in range(N_LANE_TILES):
        for r in range(0, CONV_STG_H, SUBLANES):
            uext2[L, _rows2(r), :] = jnp.where(seq_start, zero, uext2[L, _rows2(CH + r), :])
        for r in range(0, POOL_STG_H, SUBLANES):
            zext2[L, _rows2(r), :] = jnp.where(seq_start, zero, zext2[L, _rows2(CH + r), :])
    for L in range(N_LANE_TILES):
        for rt in range(n_rt):
            r0 = rt * SUBLANES
            uext2[L, _rows2(r0 + CONV_STG_H), :] = u_ref[r0:r0 + SUBLANES, _lane(L)]
            zext2[L, _rows2(r0 + POOL_STG_H), :] = z_ref[r0:r0 + SUBLANES, _lane(L)]

    conv_done = []
    for L in range(N_LANE_TILES):
        bias = jnp.broadcast_to(cb_ref[0:1, _lane(L)], (SUBLANES, LANES))
        accs = [bias] * n_rt
        for k in range(CONV_W):
            wk = cw_ref[k:k + 1, _lane(L)]
            for rt in range(n_rt):
                r = rt * SUBLANES + k + (CONV_STG_H - CONV_H)
                accs[rt] = accs[rt] + wk * uext2[L, _rows2(r), :]
        for rt in range(n_rt):
            conv_sc[rt * SUBLANES:(rt + 1) * SUBLANES, _lane(L)] = accs[rt]
        conv_done.append(accs)

    pos0 = tile_in_seq * TM + j * CH
    row = lax.broadcasted_iota(jnp.int32, (SUBLANES, LANES), 0)
    inv_cnt = {}
    for win in POOL_WINDOWS:
        for rt in range(n_rt):
            cnt = jnp.minimum(win, pos0 + rt * SUBLANES + row + 1).astype(F32)
            inv_cnt[win, rt] = 1.0 / cnt
    pool_done = []
    for L in range(N_LANE_TILES):
        win = POOL_WINDOWS[L // TILES_PER_GROUP]
        ds = []
        for rt in range(n_rt):
            base = rt * SUBLANES + POOL_STG_H
            tok = zext2[L, _rows2(base), :]
            s = tok
            for m in range(1, win):
                s = s + zext2[L, _rows2(base - m), :]
            d = s * inv_cnt[win, rt] - tok
            d_sc[rt * SUBLANES:(rt + 1) * SUBLANES, _lane(L)] = d
            ds.append(d)
        pool_done.append(ds)

    rows = pl.ds(pl.multiple_of(j * CH, CH), CH)
    c = _ln_swish(conv_sc[...], lng_ref[...], lnb_ref[...])
    mix_sc[rows, 0:D_CONV] = c.astype(BF16)
    mix_sc[rows, D_CONV:D_MODEL] = d_sc[...].astype(BF16)

    sc_ref[...] = u_ref[CH - CONV_H:CH, :]
    sp_ref[...] = z_ref[CH - POOL_H:CH, :]
    ln_done = [c[r:r + SUBLANES, _lane(L)] for r in range(0, CH, SUBLANES) for L in range(N_LANE_TILES)]
    return conv_done, pool_done, ln_done


def _tied(x, deps, zero_ref):
    if not deps:
        return x
    bits = lax.bitcast_convert_type(deps[0], jnp.int32)
    for d in deps[1:]:
        bits = bits | lax.bitcast_convert_type(d, jnp.int32)
    z = lax.bitcast_convert_type(bits & zero_ref[...], F32)[0:1, :].astype(x.dtype)
    return x + jnp.concatenate([z] * (x.shape[1] // LANES), axis=1)


def _prompt_mix_dot(mix_sc, pw_ref, ps_ref, wo_ref):
    for g in range(N_POOL_GROUPS):
        cols = slice(D_CONV + g * POOL_GROUP_W, D_CONV + (g + 1) * POOL_GROUP_W)
        mix_sc[:, cols] = _group_map(mix_sc[:, cols], g, pw_ref, ps_ref).astype(BF16)
    return jnp.dot(mix_sc[...], wo_ref[...], preferred_element_type=F32)


def _ffn_first(j, x_ref, mix_dot, g2_ref, o_ref, h2_ref):
    @pl.when(j == 0)
    def _():
        x1 = x_ref[...] + mix_dot()
        o_ref[...] = x1
        h2_ref[...] = _rms(x1, g2_ref[...]).astype(BF16)


def _ffn_tile(w1_ref, w2_ref, o_ref, h2_ref):
    f = jnp.dot(h2_ref[...], w1_ref[...], preferred_element_type=F32)
    f = jnp.square(jnp.maximum(f, 0.0)).astype(BF16)
    o_ref[...] += jnp.dot(f, w2_ref[...], preferred_element_type=F32)


def _ffn_tile_tied(w1_ref, w2_ref, o_ref, h2_ref, f_sc, after):
    n1 = TF // MXU_COLS
    for n in range(n1):
        cols = slice(n * MXU_COLS, (n + 1) * MXU_COLS)
        r = jnp.dot(h2_ref[...], after(n, w1_ref[:, cols]), preferred_element_type=F32)
        f_sc[:, cols] = jnp.square(jnp.maximum(r, 0.0)).astype(BF16)
    for n in range(D_MODEL // MXU_COLS):
        cols = slice(n * MXU_COLS, (n + 1) * MXU_COLS)
        o_ref[:, cols] += jnp.dot(f_sc[...], after(n1 + n, w2_ref[:, cols]), preferred_element_type=F32)


def _final_norm(j, o_ref, gf_ref):
    @pl.when(j == N_F - 1)
    def _():
        o_ref[...] = _rms(o_ref[...], gf_ref[...])


def _mix_ffn_kernel(n_jobs, final_norm, n_tiles,
                    u_ref, z_ref, cw_ref, cb_ref, lng_ref, lnb_ref, pw_ref, ps_ref,
                    x_ref, wo_ref, g2_ref, w1_ref, w2_ref, gf_ref, zero_ref, *rest):
    cast_src, rest = rest[:n_jobs], rest[n_jobs:]
    o_ref, sc_ref, sp_ref = rest[:3]
    cast_dst, rest = rest[3:3 + n_jobs], rest[3 + n_jobs:]
    h2_ref, mix_sc, f_sc, uext2, zext2, conv_sc, d_sc = rest
    i = pl.program_id(0)
    j = pl.program_id(1)
    tile = jnp.minimum(i, n_tiles - 1)

    _run_casts(cast_src, cast_dst)

    def mixer():
        return _mixer_chunk(tile, j, u_ref, z_ref, cw_ref, cb_ref, lng_ref, lnb_ref,
                            mix_sc, sc_ref, sp_ref, uext2, zext2, conv_sc, d_sc)

    @pl.when(i == 0)
    def _():
        @pl.when(j == 0)
        def _():
            zero = jnp.zeros((SUBLANES, LANES), F32)
            for L in range(N_LANE_TILES):
                for r in range(0, CONV_STG_H, SUBLANES):
                    uext2[L, _rows2(CH + r), :] = zero
                for r in range(0, POOL_STG_H, SUBLANES):
                    zext2[L, _rows2(CH + r), :] = zero
        mixer()

    @pl.when(i > 0)
    def _():
        mix_dot = functools.partial(_prompt_mix_dot, mix_sc, pw_ref, ps_ref, wo_ref)
        _ffn_first(j, x_ref, mix_dot, g2_ref, o_ref, h2_ref)
        conv_done, pool_done, ln_done = mixer()

        pieces = {1: conv_done[0:3], 2: conv_done[3:6], 3: conv_done[6:8], 4: pool_done, 5: [ln_done]}

        def after(b, w):
            return _tied(w, [v for piece in pieces.get(b // 2, ()) for v in piece], zero_ref)

        _ffn_tile_tied(w1_ref, w2_ref, o_ref, h2_ref, f_sc, after)
        if final_norm:
            _final_norm(j, o_ref, gf_ref)


def _mix_ffn(u, z, cw, cb, lng, lnb, pw, ps, x, wo, g2, w1, w2, gf, final_norm, cast_jobs=()):
    rows = x.shape[0]
    n_tiles = rows // TM
    tiles_per_seq = SEQ // TM
    mix_tile = lambda i: jnp.minimum(i, n_tiles - 1)
    ffn_tile = lambda i: jnp.maximum(i - 1, 0)
    ffn_j = lambda i, j: jnp.where(i > 0, j, 0)
    const2 = lambda i, j: (0, 0)
    c_in, c_out, c_shape = _cast_specs(
        cast_jobs, n_tiles * N_F, lambda i, j: ffn_tile(i) * N_F + ffn_j(i, j))
    return pl.pallas_call(
        functools.partial(_mix_ffn_kernel, len(cast_jobs), final_norm, n_tiles),
        grid=(n_tiles + 1, N_F),
        in_specs=[
            pl.BlockSpec((CH, D_CONV), lambda i, j: (mix_tile(i) * N_F + j, 0)),
            pl.BlockSpec((CH, D_POOL), lambda i, j: (mix_tile(i) * N_F + j, 0)),
            pl.BlockSpec((CONV_W, D_CONV), const2),
            pl.BlockSpec((1, D_CONV), const2),
            pl.BlockSpec((1, D_CONV), const2),
            pl.BlockSpec((1, D_CONV), const2),
            pl.BlockSpec((N_POOL_GROUPS, POOL_GROUP_W, POOL_GROUP_W), lambda i, j: (0, 0, 0)),
            pl.BlockSpec((1, D_POOL), const2),
            pl.BlockSpec((TM, D_MODEL), lambda i, j: (ffn_tile(i), 0)),
            pl.BlockSpec((D_MODEL, D_MODEL), const2, pipeline_mode=pl.Buffered(1)),
            pl.BlockSpec((1, D_MODEL), const2),
            pl.BlockSpec((D_MODEL, TF), lambda i, j: (0, ffn_j(i, j))),
            pl.BlockSpec((TF, D_MODEL), lambda i, j: (ffn_j(i, j), 0)),
            pl.BlockSpec((1, D_MODEL), const2),
            pl.BlockSpec((SUBLANES, LANES), const2),
        ] + c_in,
        out_specs=[
            pl.BlockSpec((TM, D_MODEL), lambda i, j: (ffn_tile(i), 0)),
            pl.BlockSpec((None, CONV_H, D_CONV), lambda i, j: (mix_tile(i) // tiles_per_seq, 0, 0)),
            pl.BlockSpec((None, POOL_H, D_POOL), lambda i, j: (mix_tile(i) // tiles_per_seq, 0, 0)),
        ] + c_out,
        out_shape=[
            jax.ShapeDtypeStruct((rows, D_MODEL), F32),
            jax.ShapeDtypeStruct((BATCH, CONV_H, D_CONV), F32),
            jax.ShapeDtypeStruct((BATCH, POOL_H, D_POOL), F32),
        ] + c_shape,
        scratch_shapes=[
            pltpu.VMEM((TM, D_MODEL), BF16),
            pltpu.VMEM((TM, D_MODEL), BF16),
            pltpu.VMEM((TM, TF), BF16),
            pltpu.VMEM((N_LANE_TILES, 2 * (CH + CONV_STG_H), LANES), F32),
            pltpu.VMEM((N_LANE_TILES, 2 * (CH + POOL_STG_H), LANES), F32),
            pltpu.VMEM((CH, D_CONV), F32),
            pltpu.VMEM((CH, D_POOL), F32),
        ],
        compiler_params=pltpu.CompilerParams(
            dimension_semantics=("arbitrary", "arbitrary"), vmem_limit_bytes=VMEM_LIMIT),
        name="mix_ffn",
    )(u, z, cw, cb, lng, lnb, pw, ps, x, wo, g2, w1, w2, gf, jnp.zeros((SUBLANES, LANES), jnp.int32),
      *[w_ for w_, _ in cast_jobs])


def _mix_sample_kernel(layer, u_ref, z_ref, hc_ref, hp_ref, cw_ref, cb_ref, lng_ref, lnb_ref, pw_ref, ps_ref,
                       *rest):
    if layer > 0:
        rest = rest[2:]
    mix_ref, nc_ref, np_ref = rest

    def uext(j):
        return hc_ref[j] if j < CONV_H else u_ref[j - CONV_H]

    def zext(j):
        return hp_ref[j] if j < POOL_H else z_ref[j - POOL_H]

    for t in range(DEC_SEQ):
        acc = jnp.broadcast_to(cb_ref[...], (SB, D_CONV))
        for k in range(CONV_W):
            acc = acc + cw_ref[k:k + 1, :] * uext(t + k)
        mix_ref[t, :, 0:D_CONV] = _ln_swish(acc, lng_ref[...], lnb_ref[...]).astype(BF16)

        for g in range(N_POOL_GROUPS):
            win = POOL_WINDOWS[g]
            cols = slice(g * POOL_GROUP_W, (g + 1) * POOL_GROUP_W)
            tok = zext(POOL_H + t)[:, cols]
            s = tok
            for m in range(1, win):
                s = s + zext(POOL_H + t - m)[:, cols]
            cnt = float(min(win, PAST_LEN + t + 1))
            p = _group_map(s / cnt - tok, g, pw_ref, ps_ref)
            mix_ref[t, :, D_CONV + g * POOL_GROUP_W:D_CONV + (g + 1) * POOL_GROUP_W] = p.astype(BF16)

    for j in range(CONV_H):
        nc_ref[j] = uext(j + DEC_SEQ)
    for j in range(POOL_H):
        np_ref[j] = zext(j + DEC_SEQ)


def _mix_sample(layer, u, z, hist_c, hist_p, cw, cb, lng, lnb, pw, ps, prev_states):
    tok_map = lambda s: (0, s, 0)
    st_map = lambda s: (layer, 0, s, 0)
    const2 = lambda s: (0, 0)
    n_in = 10
    return pl.pallas_call(
        functools.partial(_mix_sample_kernel, layer),
        grid=(DEC_BATCH // SB,),
        in_specs=[
            pl.BlockSpec((DEC_SEQ, SB, D_CONV), tok_map),
            pl.BlockSpec((DEC_SEQ, SB, D_POOL), tok_map),
            pl.BlockSpec((None, CONV_H, SB, D_CONV), st_map),
            pl.BlockSpec((None, POOL_H, SB, D_POOL), st_map),
            pl.BlockSpec((CONV_W, D_CONV), const2),
            pl.BlockSpec((1, D_CONV), const2),
            pl.BlockSpec((1, D_CONV), const2),
            pl.BlockSpec((1, D_CONV), const2),
            pl.BlockSpec((N_POOL_GROUPS, POOL_GROUP_W, POOL_GROUP_W), lambda s: (0, 0, 0)),
            pl.BlockSpec((1, D_POOL), const2),
        ] + [pl.BlockSpec(memory_space=pl.ANY)] * len(prev_states),
        out_specs=[
            pl.BlockSpec((DEC_SEQ, SB, D_MODEL), tok_map),
            pl.BlockSpec((None, CONV_H, SB, D_CONV), st_map),
            pl.BlockSpec((None, POOL_H, SB, D_POOL), st_map),
        ],
        out_shape=[
            jax.ShapeDtypeStruct((DEC_SEQ, DEC_BATCH, D_MODEL), BF16),
            jax.ShapeDtypeStruct((DEPTH, CONV_H, DEC_BATCH, D_CONV), F32),
            jax.ShapeDtypeStruct((DEPTH, POOL_H, DEC_BATCH, D_POOL), F32),
        ],
        input_output_aliases={n_in + k: 1 + k for k in range(len(prev_states))},
        compiler_params=pltpu.CompilerParams(
            dimension_semantics=("arbitrary",), vmem_limit_bytes=VMEM_LIMIT),
        name="mix_sample",
    )(u, z, hist_c, hist_p, cw, cb, lng, lnb, pw, ps, *prev_states)


def _out_ffn_kernel(final_norm, x_ref, mix_ref, wo_ref, g2_ref, w1_ref, w2_ref, gf_ref, o_ref, h2_ref):
    j = pl.program_id(1)
    mix_dot = lambda: jnp.dot(mix_ref[...], wo_ref[...], preferred_element_type=F32)
    _ffn_first(j, x_ref, mix_dot, g2_ref, o_ref, h2_ref)
    _ffn_tile(w1_ref, w2_ref, o_ref, h2_ref)
    if final_norm:
        _final_norm(j, o_ref, gf_ref)


def _out_ffn(x, mix, wo, g2, w1, w2, gf, final_norm):
    rows = x.shape[0]
    return pl.pallas_call(
        functools.partial(_out_ffn_kernel, final_norm),
        grid=(rows // TM, N_F),
        in_specs=[
            pl.BlockSpec((TM, D_MODEL), lambda i, j: (i, 0)),
            pl.BlockSpec((TM, D_MODEL), lambda i, j: (i, 0)),
            pl.BlockSpec((D_MODEL, D_MODEL), lambda i, j: (0, 0), pipeline_mode=pl.Buffered(1)),
            pl.BlockSpec((1, D_MODEL), lambda i, j: (0, 0)),
            pl.BlockSpec((D_MODEL, TF), lambda i, j: (0, j)),
            pl.BlockSpec((TF, D_MODEL), lambda i, j: (j, 0)),
            pl.BlockSpec((1, D_MODEL), lambda i, j: (0, 0)),
        ],
        out_specs=pl.BlockSpec((TM, D_MODEL), lambda i, j: (i, 0)),
        out_shape=jax.ShapeDtypeStruct((rows, D_MODEL), F32),
        scratch_shapes=[pltpu.VMEM((TM, D_MODEL), BF16)],
        compiler_params=pltpu.CompilerParams(
            dimension_semantics=("arbitrary", "arbitrary"), vmem_limit_bytes=VMEM_LIMIT),
        name="out_ffn",
    )(x, mix, wo, g2, w1, w2, gf)


def kernel(x_prompt, x_sample, state_conv, state_pool, norm1_g, w_in, b_in, conv_w, conv_b, ln_g, ln_b, pool_w, pool_scale, w_out, norm2_g, w_ff1, w_ff2, norm_f):
    xp = x_prompt.reshape(BATCH * SEQ, D_MODEL)
    xs = x_sample.transpose(1, 0, 2).reshape(DEC_SEQ * DEC_BATCH, D_MODEL)
    hist_c = state_conv.transpose(0, 2, 1, 3)
    hist_p = state_pool.transpose(0, 2, 1, 3)
    pool_w_b = pool_w.astype(BF16)
    w_in_b, w_out_b = w_in[0].astype(BF16), w_out[0].astype(BF16)
    w_ff1_b = w_ff2_b = None

    conv_p, pool_p = [], []
    sample_states = ()
    for l in range(DEPTH):
        row = lambda a: a[l][None, :]
        mixer_w = (conv_w[l], row(conv_b), row(ln_g), row(ln_b), pool_w_b[l], row(pool_scale))
        final = l == DEPTH - 1
        as3 = lambda a: a.reshape(DEC_SEQ, DEC_BATCH, a.shape[-1])

        jobs = ((w_ff1, 0), (w_ff2, 0)) if l == 0 else ()
        up, zp, *cast = _in_proj(xp, row(norm1_g), w_in_b, row(b_in), cast_jobs=jobs)
        if l == 0:
            w_ff1_b, w_ff2_b = cast
        us, zs = _in_proj(xs, row(norm1_g), w_in_b, row(b_in))
        mix_s, *sample_states = _mix_sample(l, as3(us), as3(zs), hist_c, hist_p, *mixer_w, sample_states)
        ffn_w = (w_out_b, row(norm2_g), w_ff1_b, w_ff2_b, norm_f[None, :])
        xs = _out_ffn(xs, mix_s.reshape(DEC_SEQ * DEC_BATCH, D_MODEL), *ffn_w, final)
        jobs = () if final else ((w_in, l + 1), (w_out, l + 1), (w_ff1, l + 1), (w_ff2, l + 1))
        xp, cp, pp, *cast = _mix_ffn(up, zp, *mixer_w, xp, *ffn_w, final, cast_jobs=jobs)
        if not final:
            w_in_b, w_out_b, w_ff1_b, w_ff2_b = cast
        conv_p.append(cp)
        pool_p.append(pp)

    conv_s, pool_s = sample_states
    y_sample = xs.reshape(DEC_SEQ, DEC_BATCH, D_MODEL).transpose(1, 0, 2)
    return (xp.reshape(BATCH, SEQ, D_MODEL), y_sample, jnp.stack(conv_p), jnp.stack(pool_p),
            conv_s.transpose(0, 2, 1, 3), pool_s.transpose(0, 2, 1, 3))
```

```python
import functools

import jax
import jax.numpy as jnp
from jax import lax
from jax.experimental import pallas as pl
from jax.experimental.pallas import tpu as pltpu

F32 = jnp.float32
BF16 = jnp.bfloat16

D_MODEL = 2048
BATCH = 4
SEQ = 2048
DEPTH = 2
DEC_BATCH = 128
DEC_SEQ = 4
PAST_LEN = 16384
D_CONV = 1024
D_POOL = 1024
N_POOL_GROUPS = 4
POOL_GROUP_W = D_POOL // N_POOL_GROUPS
POOL_WINDOWS = (2, 4, 8, 16)
POOL_MAX = 16
CONV_W = 31
D_FF = 4 * D_MODEL
D_IN = 2 * D_CONV + D_POOL
EPS = 1e-6

CONV_H = CONV_W - 1
POOL_H = POOL_MAX - 1

LANES = 128
SUBLANES = 8
BF16_ROWS = 16
MXU_COLS = 256
N_LANE_TILES = D_CONV // LANES
TILES_PER_GROUP = N_LANE_TILES // N_POOL_GROUPS
VMEM_LIMIT = 60 * 1024 * 1024

TM = 512
TF = 1024
N_F = D_FF // TF
CH = TM // N_F
CONV_STG_H = 32
POOL_STG_H = 16
SB = 32
assert SEQ % TM == 0 and CH % BF16_ROWS == 0 and CH >= CONV_STG_H


def _lane(L):
    return slice(L * LANES, (L + 1) * LANES)


def _rows2(r, n=SUBLANES):
    return pl.ds(2 * r, n, stride=2)


def _rms(x, g):
    return x * lax.rsqrt(jnp.mean(x * x, axis=-1, keepdims=True) + EPS) * g


def _ln_swish(x, g, b):
    mu = jnp.mean(x, axis=-1, keepdims=True)
    xc = x - mu
    var = jnp.mean(xc * xc, axis=-1, keepdims=True)
    y = xc * lax.rsqrt(var + EPS) * g + b
    return y * jax.nn.sigmoid(y)


def _group_map(d, g, pw_ref, ps_ref):
    cols = slice(g * POOL_GROUP_W, (g + 1) * POOL_GROUP_W)
    return jnp.dot(d.astype(BF16), pw_ref[g], preferred_element_type=F32) * ps_ref[:, cols]


def _cast_specs(jobs, n_steps, step_of):
    in_specs, out_specs, out_shapes = [], [], []
    for w, layer in jobs:
        _, r, c = w.shape
        rows = r // n_steps
        assert rows * n_steps == r and rows % BF16_ROWS == 0
        in_specs.append(pl.BlockSpec(
            (None, rows, c), lambda *ids, layer=layer: (layer, step_of(*ids), 0)))
        out_specs.append(pl.BlockSpec((rows, c), lambda *ids: (step_of(*ids), 0)))
        out_shapes.append(jax.ShapeDtypeStruct((r, c), BF16))
    return in_specs, out_specs, out_shapes


def _run_casts(src_refs, dst_refs):
    for src, dst in zip(src_refs, dst_refs):
        dst[...] = src[...].astype(BF16)


def _in_proj_kernel(n_jobs, x_ref, g_ref, w_ref, b_ref, *rest):
    cast_src, rest = rest[:n_jobs], rest[n_jobs:]
    u_ref, z_ref = rest[:2]
    _run_casts(cast_src, rest[2:])
    h = _rms(x_ref[...], g_ref[...]).astype(BF16)

    def proj(lo, hi):
        return jnp.dot(h, w_ref[:, lo:hi], preferred_element_type=F32) + b_ref[:, lo:hi]

    a = proj(0, D_CONV)
    gate = proj(D_CONV, 2 * D_CONV)
    u_ref[...] = a * jax.nn.sigmoid(gate)
    z_ref[...] = proj(2 * D_CONV, D_IN)


def _in_proj(x, g, w, b, cast_jobs=()):
    rows = x.shape[0]
    c_in, c_out, c_shape = _cast_specs(cast_jobs, rows // TM, lambda i: i)
    return pl.pallas_call(
        functools.partial(_in_proj_kernel, len(cast_jobs)),
        grid=(rows // TM,),
        in_specs=[
            pl.BlockSpec((TM, D_MODEL), lambda i: (i, 0)),
            pl.BlockSpec((1, D_MODEL), lambda i: (0, 0)),
            pl.BlockSpec((D_MODEL, D_IN), lambda i: (0, 0), pipeline_mode=pl.Buffered(1)),
            pl.BlockSpec((1, D_IN), lambda i: (0, 0)),
        ] + c_in,
        out_specs=[
            pl.BlockSpec((TM, D_CONV), lambda i: (i, 0)),
            pl.BlockSpec((TM, D_POOL), lambda i: (i, 0)),
        ] + c_out,
        out_shape=[
            jax.ShapeDtypeStruct((rows, D_CONV), F32),
            jax.ShapeDtypeStruct((rows, D_POOL), F32),
        ] + c_shape,
        compiler_params=pltpu.CompilerParams(
            dimension_semantics=("arbitrary",), vmem_limit_bytes=VMEM_LIMIT),
        name="in_proj",
    )(x, g, w, b, *[w_ for w_, _ in cast_jobs])


def _mixer_chunk(tile, j, u_ref, z_ref, cw_ref, cb_ref, lng_ref, lnb_ref,
                 mix_sc, sc_ref, sp_ref, uext2, zext2, conv_sc, d_sc):
    tile_in_seq = tile % (SEQ // TM)
    seq_start = jnp.logical_and(tile_in_seq == 0, j == 0)
    zero = jnp.zeros((SUBLANES, LANES), F32)
    n_rt = CH // SUBLANES

    for L in range(N_LANE_TILES):
        for r in range(0, CONV_STG_H, SUBLANES):
            uext2[L, _rows2(r), :] = jnp.where(seq_start, zero, uext2[L, _rows2(CH + r), :])
        for r in range(0, POOL_STG_H, SUBLANES):
            zext2[L, _rows2(r), :] = jnp.where(seq_start, zero, zext2[L, _rows2(CH + r), :])
    for L in range(N_LANE_TILES):
        for rt in range(n_rt):
            r0 = rt * SUBLANES
            uext2[L, _rows2(r0 + CONV_STG_H), :] = u_ref[r0:r0 + SUBLANES, _lane(L)]
            zext2[L, _rows2(r0 + POOL_STG_H), :] = z_ref[r0:r0 + SUBLANES, _lane(L)]

    conv_done = []
    for L in range(N_LANE_TILES):
        bias = jnp.broadcast_to(cb_ref[0:1, _lane(L)], (SUBLANES, LANES))
        accs = [bias] * n_rt
        for k in range(CONV_W):
            wk = cw_ref[k:k + 1, _lane(L)]
            for rt in range(n_rt):
                r = rt * SUBLANES + k + (CONV_STG_H - CONV_H)
                accs[rt] = accs[rt] + wk * uext2[L, _rows2(r), :]
        for rt in range(n_rt):
            conv_sc[rt * SUBLANES:(rt + 1) * SUBLANES, _lane(L)] = accs[rt]
        conv_done.append(accs)

    pos0 = tile_in_seq * TM + j * CH
    row = lax.broadcasted_iota(jnp.int32, (SUBLANES, LANES), 0)
    inv_cnt = {}
    for win in POOL_WINDOWS:
        for rt in range(n_rt):
            cnt = jnp.minimum(win, pos0 + rt * SUBLANES + row + 1).astype(F32)
            inv_cnt[win, rt] = 1.0 / cnt
    pool_done = []
    for L in range(N_LANE_TILES):
        win = POOL_WINDOWS[L // TILES_PER_GROUP]
        ds = []
        for rt in range(n_rt):
            base = rt * SUBLANES + POOL_STG_H
            tok = zext2[L, _rows2(base), :]
            s = tok
            for m in range(1, win):
                s = s + zext2[L, _rows2(base - m), :]
            d = s * inv_cnt[win, rt] - tok
            d_sc[rt * SUBLANES:(rt + 1) * SUBLANES, _lane(L)] = d
            ds.append(d)
        pool_done.append(ds)

    rows = pl.ds(pl.multiple_of(j * CH, CH), CH)
    c = _ln_swish(conv_sc[...], lng_ref[...], lnb_ref[...])
    mix_sc[rows, 0:D_CONV] = c.astype(BF16)
    mix_sc[rows, D_CONV:D_MODEL] = d_sc[...].astype(BF16)

    sc_ref[...] = u_ref[CH - CONV_H:CH, :]
    sp_ref[...] = z_ref[CH - POOL_H:CH, :]
    ln_done = [c[r:r + SUBLANES, _lane(L)] for r in range(0, CH, SUBLANES) for L in range(N_LANE_TILES)]
    return conv_done, pool_done, ln_done


def _tied(x, deps, zero_ref):
    if not deps:
        return x
    bits = [lax.bitcast_convert_type(d, jnp.int32) for d in deps]
    while len(bits) > 1:
        bits = [a | b for a, b in zip(bits[0::2], bits[1::2])] + bits[len(bits) & ~1:]
    z = lax.bitcast_convert_type(bits[0] & zero_ref[...], F32)[0:1, :].astype(x.dtype)
    return x + jnp.concatenate([z] * (x.shape[1] // LANES), axis=1)


def _prompt_mix_dot(mix_sc, pw_ref, ps_ref, wo_ref):
    for g in range(N_POOL_GROUPS):
        cols = slice(D_CONV + g * POOL_GROUP_W, D_CONV + (g + 1) * POOL_GROUP_W)
        mix_sc[:, cols] = _group_map(mix_sc[:, cols], g, pw_ref, ps_ref).astype(BF16)
    return jnp.dot(mix_sc[...], wo_ref[...], preferred_element_type=F32)


def _ffn_first(j, x_ref, mix_dot, g2_ref, o_ref, h2_ref):
    @pl.when(j == 0)
    def _():
        x1 = x_ref[...] + mix_dot()
        o_ref[...] = x1
        h2_ref[...] = _rms(x1, g2_ref[...]).astype(BF16)


def _ffn_tile(w1_ref, w2_ref, o_ref, h2_ref):
    f = jnp.dot(h2_ref[...], w1_ref[...], preferred_element_type=F32)
    f = jnp.square(jnp.maximum(f, 0.0)).astype(BF16)
    o_ref[...] += jnp.dot(f, w2_ref[...], preferred_element_type=F32)


def _ffn_tile_tied(w1_ref, w2_ref, o_ref, h2_ref, f_sc, after):
    n1 = TF // MXU_COLS
    for n in range(n1):
        cols = slice(n * MXU_COLS, (n + 1) * MXU_COLS)
        r = jnp.dot(h2_ref[...], after(n, w1_ref[:, cols]), preferred_element_type=F32)
        f_sc[:, cols] = jnp.square(jnp.maximum(r, 0.0)).astype(BF16)
    for n in range(D_MODEL // MXU_COLS):
        cols = slice(n * MXU_COLS, (n + 1) * MXU_COLS)
        o_ref[:, cols] += jnp.dot(f_sc[...], after(n1 + n, w2_ref[:, cols]), preferred_element_type=F32)


def _final_norm(j, o_ref, gf_ref):
    @pl.when(j == N_F - 1)
    def _():
        o_ref[...] = _rms(o_ref[...], gf_ref[...])


def _mix_ffn_kernel(n_jobs, final_norm, n_tiles,
                    u_ref, z_ref, cw_ref, cb_ref, lng_ref, lnb_ref, pw_ref, ps_ref,
                    x_ref, wo_ref, g2_ref, w1_ref, w2_ref, gf_ref, zero_ref, *rest):
    cast_src, rest = rest[:n_jobs], rest[n_jobs:]
    o_ref, sc_ref, sp_ref = rest[:3]
    cast_dst, rest = rest[3:3 + n_jobs], rest[3 + n_jobs:]
    h2_ref, mix_sc, f_sc, uext2, zext2, conv_sc, d_sc = rest
    i = pl.program_id(0)
    j = pl.program_id(1)
    tile = jnp.minimum(i, n_tiles - 1)

    _run_casts(cast_src, cast_dst)

    def mixer():
        return _mixer_chunk(tile, j, u_ref, z_ref, cw_ref, cb_ref, lng_ref, lnb_ref,
                            mix_sc, sc_ref, sp_ref, uext2, zext2, conv_sc, d_sc)

    @pl.when(i == 0)
    def _():
        @pl.when(j == 0)
        def _():
            zero = jnp.zeros((SUBLANES, LANES), F32)
            for L in range(N_LANE_TILES):
                for r in range(0, CONV_STG_H, SUBLANES):
                    uext2[L, _rows2(CH + r), :] = zero
                for r in range(0, POOL_STG_H, SUBLANES):
                    zext2[L, _rows2(CH + r), :] = zero
        mixer()

    @pl.when(i > 0)
    def _():
        mix_dot = functools.partial(_prompt_mix_dot, mix_sc, pw_ref, ps_ref, wo_ref)
        _ffn_first(j, x_ref, mix_dot, g2_ref, o_ref, h2_ref)
        conv_done, pool_done, ln_done = mixer()

        pieces = {0: conv_done[0:3], 1: conv_done[3:6], 2: conv_done[6:8], 3: pool_done, 4: [ln_done]}

        def after(b, w):
            deps = [v for piece in pieces.get(b // 2, ()) for v in piece]
            if not deps:
                return w
            k_last = w.shape[0] - MXU_COLS
            return jnp.concatenate([w[:k_last], _tied(w[k_last:], deps, zero_ref)], axis=0)

        del after
        _ffn_tile(w1_ref, w2_ref, o_ref, h2_ref)
        if final_norm:
            _final_norm(j, o_ref, gf_ref)


def _mix_ffn(u, z, cw, cb, lng, lnb, pw, ps, x, wo, g2, w1, w2, gf, final_norm, cast_jobs=()):
    rows = x.shape[0]
    n_tiles = rows // TM
    tiles_per_seq = SEQ // TM
    mix_tile = lambda i: jnp.minimum(i, n_tiles - 1)
    ffn_tile = lambda i: jnp.maximum(i - 1, 0)
    ffn_j = lambda i, j: jnp.where(i > 0, j, 0)
    const2 = lambda i, j: (0, 0)
    c_in, c_out, c_shape = _cast_specs(
        cast_jobs, n_tiles * N_F, lambda i, j: ffn_tile(i) * N_F + ffn_j(i, j))
    return pl.pallas_call(
        functools.partial(_mix_ffn_kernel, len(cast_jobs), final_norm, n_tiles),
        grid=(n_tiles + 1, N_F),
        in_specs=[
            pl.BlockSpec((CH, D_CONV), lambda i, j: (mix_tile(i) * N_F + j, 0)),
            pl.BlockSpec((CH, D_POOL), lambda i, j: (mix_tile(i) * N_F + j, 0)),
            pl.BlockSpec((CONV_W, D_CONV), const2),
            pl.BlockSpec((1, D_CONV), const2),
            pl.BlockSpec((1, D_CONV), const2),
            pl.BlockSpec((1, D_CONV), const2),
            pl.BlockSpec((N_POOL_GROUPS, POOL_GROUP_W, POOL_GROUP_W), lambda i, j: (0, 0, 0)),
            pl.BlockSpec((1, D_POOL), const2),
            pl.BlockSpec((TM, D_MODEL), lambda i, j: (ffn_tile(i), 0)),
            pl.BlockSpec((D_MODEL, D_MODEL), const2, pipeline_mode=pl.Buffered(1)),
            pl.BlockSpec((1, D_MODEL), const2),
            pl.BlockSpec((D_MODEL, TF), lambda i, j: (0, ffn_j(i, j))),
            pl.BlockSpec((TF, D_MODEL), lambda i, j: (ffn_j(i, j), 0)),
            pl.BlockSpec((1, D_MODEL), const2),
            pl.BlockSpec((SUBLANES, LANES), const2),
        ] + c_in,
        out_specs=[
            pl.BlockSpec((TM, D_MODEL), lambda i, j: (ffn_tile(i), 0)),
            pl.BlockSpec((None, CONV_H, D_CONV), lambda i, j: (mix_tile(i) // tiles_per_seq, 0, 0)),
            pl.BlockSpec((None, POOL_H, D_POOL), lambda i, j: (mix_tile(i) // tiles_per_seq, 0, 0)),
        ] + c_out,
        out_shape=[
            jax.ShapeDtypeStruct((rows, D_MODEL), F32),
            jax.ShapeDtypeStruct((BATCH, CONV_H, D_CONV), F32),
            jax.ShapeDtypeStruct((BATCH, POOL_H, D_POOL), F32),
        ] + c_shape,
        scratch_shapes=[
            pltpu.VMEM((TM, D_MODEL), BF16),
            pltpu.VMEM((TM, D_MODEL), BF16),
            pltpu.VMEM((TM, TF), BF16),
            pltpu.VMEM((N_LANE_TILES, 2 * (CH + CONV_STG_H), LANES), F32),
            pltpu.VMEM((N_LANE_TILES, 2 * (CH + POOL_STG_H), LANES), F32),
            pltpu.VMEM((CH, D_CONV), F32),
            pltpu.VMEM((CH, D_POOL), F32),
        ],
        compiler_params=pltpu.CompilerParams(
            dimension_semantics=("arbitrary", "arbitrary"), vmem_limit_bytes=VMEM_LIMIT),
        name="mix_ffn",
    )(u, z, cw, cb, lng, lnb, pw, ps, x, wo, g2, w1, w2, gf, jnp.zeros((SUBLANES, LANES), jnp.int32),
      *[w_ for w_, _ in cast_jobs])


def _mix_sample_kernel(layer, u_ref, z_ref, hc_ref, hp_ref, cw_ref, cb_ref, lng_ref, lnb_ref, pw_ref, ps_ref,
                       *rest):
    if layer > 0:
        rest = rest[2:]
    mix_ref, nc_ref, np_ref = rest

    def uext(j):
        return hc_ref[j] if j < CONV_H else u_ref[j - CONV_H]

    def zext(j):
        return hp_ref[j] if j < POOL_H else z_ref[j - POOL_H]

    for t in range(DEC_SEQ):
        acc = jnp.broadcast_to(cb_ref[...], (SB, D_CONV))
        for k in range(CONV_W):
            acc = acc + cw_ref[k:k + 1, :] * uext(t + k)
        mix_ref[t, :, 0:D_CONV] = _ln_swish(acc, lng_ref[...], lnb_ref[...]).astype(BF16)

        for g in range(N_POOL_GROUPS):
            win = POOL_WINDOWS[g]
            cols = slice(g * POOL_GROUP_W, (g + 1) * POOL_GROUP_W)
            tok = zext(POOL_H + t)[:, cols]
            s = tok
            for m in range(1, win):
                s = s + zext(POOL_H + t - m)[:, cols]
            cnt = float(min(win, PAST_LEN + t + 1))
            p = _group_map(s / cnt - tok, g, pw_ref, ps_ref)
            mix_ref[t, :, D_CONV + g * POOL_GROUP_W:D_CONV + (g + 1) * POOL_GROUP_W] = p.astype(BF16)

    for j in range(CONV_H):
        nc_ref[j] = uext(j + DEC_SEQ)
    for j in range(POOL_H):
        np_ref[j] = zext(j + DEC_SEQ)


def _mix_sample(layer, u, z, hist_c, hist_p, cw, cb, lng, lnb, pw, ps, prev_states):
    tok_map = lambda s: (0, s, 0)
    st_map = lambda s: (layer, 0, s, 0)
    const2 = lambda s: (0, 0)
    n_in = 10
    return pl.pallas_call(
        functools.partial(_mix_sample_kernel, layer),
        grid=(DEC_BATCH // SB,),
        in_specs=[
            pl.BlockSpec((DEC_SEQ, SB, D_CONV), tok_map),
            pl.BlockSpec((DEC_SEQ, SB, D_POOL), tok_map),
            pl.BlockSpec((None, CONV_H, SB, D_CONV), st_map),
            pl.BlockSpec((None, POOL_H, SB, D_POOL), st_map),
            pl.BlockSpec((CONV_W, D_CONV), const2),
            pl.BlockSpec((1, D_CONV), const2),
            pl.BlockSpec((1, D_CONV), const2),
            pl.BlockSpec((1, D_CONV), const2),
            pl.BlockSpec((N_POOL_GROUPS, POOL_GROUP_W, POOL_GROUP_W), lambda s: (0, 0, 0)),
            pl.BlockSpec((1, D_POOL), const2),
        ] + [pl.BlockSpec(memory_space=pl.ANY)] * len(prev_states),
        out_specs=[
            pl.BlockSpec((DEC_SEQ, SB, D_MODEL), tok_map),
            pl.BlockSpec((None, CONV_H, SB, D_CONV), st_map),
            pl.BlockSpec((None, POOL_H, SB, D_POOL), st_map),
        ],
        out_shape=[
            jax.ShapeDtypeStruct((DEC_SEQ, DEC_BATCH, D_MODEL), BF16),
            jax.ShapeDtypeStruct((DEPTH, CONV_H, DEC_BATCH, D_CONV), F32),
            jax.ShapeDtypeStruct((DEPTH, POOL_H, DEC_BATCH, D_POOL), F32),
        ],
        input_output_aliases={n_in + k: 1 + k for k in range(len(prev_states))},
        compiler_params=pltpu.CompilerParams(
            dimension_semantics=("arbitrary",), vmem_limit_bytes=VMEM_LIMIT),
        name="mix_sample",
    )(u, z, hist_c, hist_p, cw, cb, lng, lnb, pw, ps, *prev_states)


def _out_ffn_kernel(final_norm, x_ref, mix_ref, wo_ref, g2_ref, w1_ref, w2_ref, gf_ref, o_ref, h2_ref):
    j = pl.program_id(1)
    mix_dot = lambda: jnp.dot(mix_ref[...], wo_ref[...], preferred_element_type=F32)
    _ffn_first(j, x_ref, mix_dot, g2_ref, o_ref, h2_ref)
    _ffn_tile(w1_ref, w2_ref, o_ref, h2_ref)
    if final_norm:
        _final_norm(j, o_ref, gf_ref)


def _out_ffn(x, mix, wo, g2, w1, w2, gf, final_norm):
    rows = x.shape[0]
    return pl.pallas_call(
        functools.partial(_out_ffn_kernel, final_norm),
        grid=(rows // TM, N_F),
        in_specs=[
            pl.BlockSpec((TM, D_MODEL), lambda i, j: (i, 0)),
            pl.BlockSpec((TM, D_MODEL), lambda i, j: (i, 0)),
            pl.BlockSpec((D_MODEL, D_MODEL), lambda i, j: (0, 0), pipeline_mode=pl.Buffered(1)),
            pl.BlockSpec((1, D_MODEL), lambda i, j: (0, 0)),
            pl.BlockSpec((D_MODEL, TF), lambda i, j: (0, j)),
            pl.BlockSpec((TF, D_MODEL), lambda i, j: (j, 0)),
            pl.BlockSpec((1, D_MODEL), lambda i, j: (0, 0)),
        ],
        out_specs=pl.BlockSpec((TM, D_MODEL), lambda i, j: (i, 0)),
        out_shape=jax.ShapeDtypeStruct((rows, D_MODEL), F32),
        scratch_shapes=[pltpu.VMEM((TM, D_MODEL), BF16)],
        compiler_params=pltpu.CompilerParams(
            dimension_semantics=("arbitrary", "arbitrary"), vmem_limit_bytes=VMEM_LIMIT),
        name="out_ffn",
    )(x, mix, wo, g2, w1, w2, gf)


def kernel(x_prompt, x_sample, state_conv, state_pool, norm1_g, w_in, b_in, conv_w, conv_b, ln_g, ln_b, pool_w, pool_scale, w_out, norm2_g, w_ff1, w_ff2, norm_f):
    xp = x_prompt.reshape(BATCH * SEQ, D_MODEL)
    xs = x_sample.transpose(1, 0, 2).reshape(DEC_SEQ * DEC_BATCH, D_MODEL)
    hist_c = state_conv.transpose(0, 2, 1, 3)
    hist_p = state_pool.transpose(0, 2, 1, 3)
    pool_w_b = pool_w.astype(BF16)
    w_in_b, w_out_b = w_in[0].astype(BF16), w_out[0].astype(BF16)
    w_ff1_b = w_ff2_b = None

    conv_p, pool_p = [], []
    sample_states = ()
    for l in range(DEPTH):
        row = lambda a: a[l][None, :]
        mixer_w = (conv_w[l], row(conv_b), row(ln_g), row(ln_b), pool_w_b[l], row(pool_scale))
        final = l == DEPTH - 1
        as3 = lambda a: a.reshape(DEC_SEQ, DEC_BATCH, a.shape[-1])

        jobs = ((w_ff1, 0), (w_ff2, 0)) if l == 0 else ()
        up, zp, *cast = _in_proj(xp, row(norm1_g), w_in_b, row(b_in), cast_jobs=jobs)
        if l == 0:
            w_ff1_b, w_ff2_b = cast
        us, zs = _in_proj(xs, row(norm1_g), w_in_b, row(b_in))
        mix_s, *sample_states = _mix_sample(l, as3(us), as3(zs), hist_c, hist_p, *mixer_w, sample_states)
        ffn_w = (w_out_b, row(norm2_g), w_ff1_b, w_ff2_b, norm_f[None, :])
        xs = _out_ffn(xs, mix_s.reshape(DEC_SEQ * DEC_BATCH, D_MODEL), *ffn_w, final)
        jobs = () if final else ((w_in, l + 1), (w_out, l + 1), (w_ff1, l + 1), (w_ff2, l + 1))
        xp, cp, pp, *cast = _mix_ffn(up, zp, *mixer_w, xp, *ffn_w, final, cast_jobs=jobs)
        if not final:
            w_in_b, w_out_b, w_ff1_b, w_ff2_b = cast
        conv_p.append(cp)
        pool_p.append(pp)

    conv_s, pool_s = sample_states
    y_sample = xs.reshape(DEC_SEQ, DEC_BATCH, D_MODEL).transpose(1, 0, 2)
    return (xp.reshape(BATCH, SEQ, D_MODEL), y_sample, jnp.stack(conv_p), jnp.stack(pool_p),
            conv_s.transpose(0, 2, 1, 3), pool_s.transpose(0, 2, 1, 3))
```

```python
import functools

import jax
import jax.numpy as jnp
from jax import lax
from jax.experimental import pallas as pl
from jax.experimental.pallas import tpu as pltpu

F32 = jnp.float32
BF16 = jnp.bfloat16

D_MODEL = 2048
BATCH = 4
SEQ = 2048
DEPTH = 2
DEC_BATCH = 128
DEC_SEQ = 4
PAST_LEN = 16384
D_CONV = 1024
D_POOL = 1024
N_POOL_GROUPS = 4
POOL_GROUP_W = D_POOL // N_POOL_GROUPS
POOL_WINDOWS = (2, 4, 8, 16)
POOL_MAX = 16
CONV_W = 31
D_FF = 4 * D_MODEL
D_IN = 2 * D_CONV + D_POOL
EPS = 1e-6

CONV_H = CONV_W - 1
POOL_H = POOL_MAX - 1

LANES = 128
SUBLANES = 8
BF16_ROWS = 16
MXU_COLS = 256
N_LANE_TILES = D_CONV // LANES
TILES_PER_GROUP = N_LANE_TILES // N_POOL_GROUPS
VMEM_LIMIT = 60 * 1024 * 1024

TM = 512
TF = 1024
N_F = D_FF // TF
CH = TM // N_F
CONV_STG_H = 32
POOL_STG_H = 16
SB = 32
assert SEQ % TM == 0 and CH % BF16_ROWS == 0 and CH >= CONV_STG_H


def _lane(L):
    return slice(L * LANES, (L + 1) * LANES)


def _rows2(r, n=SUBLANES):
    return pl.ds(2 * r, n, stride=2)


def _rms(x, g):
    return x * lax.rsqrt(jnp.mean(x * x, axis=-1, keepdims=True) + EPS) * g


def _ln_swish(x, g, b):
    mu = jnp.mean(x, axis=-1, keepdims=True)
    xc = x - mu
    var = jnp.mean(xc * xc, axis=-1, keepdims=True)
    y = xc * lax.rsqrt(var + EPS) * g + b
    return y * jax.nn.sigmoid(y)


def _layer_spec(a, layer):
    zeros = (0,) * (a.ndim - 1)
    return pl.BlockSpec((None,) + a.shape[1:], lambda *ids: (layer,) + zeros)


def _group_map(d, g, pw_ref, ps_ref):
    cols = slice(g * POOL_GROUP_W, (g + 1) * POOL_GROUP_W)
    return jnp.dot(d.astype(BF16), pw_ref[g], preferred_element_type=F32) * ps_ref[:, cols]


def _cast_specs(jobs, n_steps, step_of):
    in_specs, out_specs, out_shapes = [], [], []
    for w, layer in jobs:
        _, r, c = w.shape
        rows = r // n_steps
        assert rows * n_steps == r and rows % BF16_ROWS == 0
        in_specs.append(pl.BlockSpec(
            (None, rows, c), lambda *ids, layer=layer: (layer, step_of(*ids), 0)))
        out_specs.append(pl.BlockSpec((rows, c), lambda *ids: (step_of(*ids), 0)))
        out_shapes.append(jax.ShapeDtypeStruct((r, c), BF16))
    return in_specs, out_specs, out_shapes


def _run_casts(src_refs, dst_refs):
    for src, dst in zip(src_refs, dst_refs):
        dst[...] = src[...].astype(BF16)


def _in_proj_kernel(n_jobs, x_ref, g_ref, w_ref, b_ref, *rest):
    cast_src, rest = rest[:n_jobs], rest[n_jobs:]
    u_ref, z_ref = rest[:2]
    _run_casts(cast_src, rest[2:])
    h = _rms(x_ref[...], g_ref[...]).astype(BF16)

    def proj(lo, hi):
        return jnp.dot(h, w_ref[:, lo:hi], preferred_element_type=F32) + b_ref[:, lo:hi]

    a = proj(0, D_CONV)
    gate = proj(D_CONV, 2 * D_CONV)
    u_ref[...] = a * jax.nn.sigmoid(gate)
    z_ref[...] = proj(2 * D_CONV, D_IN)


def _in_proj(layer, x, g, w, b, cast_jobs=()):
    rows = x.shape[0]
    c_in, c_out, c_shape = _cast_specs(cast_jobs, rows // TM, lambda i: i)
    return pl.pallas_call(
        functools.partial(_in_proj_kernel, len(cast_jobs)),
        grid=(rows // TM,),
        in_specs=[
            pl.BlockSpec((TM, D_MODEL), lambda i: (i, 0)),
            _layer_spec(g, layer),
            pl.BlockSpec((D_MODEL, D_IN), lambda i: (0, 0), pipeline_mode=pl.Buffered(1)),
            _layer_spec(b, layer),
        ] + c_in,
        out_specs=[
            pl.BlockSpec((TM, D_CONV), lambda i: (i, 0)),
            pl.BlockSpec((TM, D_POOL), lambda i: (i, 0)),
        ] + c_out,
        out_shape=[
            jax.ShapeDtypeStruct((rows, D_CONV), F32),
            jax.ShapeDtypeStruct((rows, D_POOL), F32),
        ] + c_shape,
        compiler_params=pltpu.CompilerParams(
            dimension_semantics=("arbitrary",), vmem_limit_bytes=VMEM_LIMIT),
        name="in_proj",
    )(x, g, w, b, *[w_ for w_, _ in cast_jobs])


def _mixer_chunk(tile, j, u_ref, z_ref, cw_ref, cb_ref, lng_ref, lnb_ref,
                 mix_sc, sc_ref, sp_ref, uext2, zext2, conv_sc, d_sc):
    tile_in_seq = tile % (SEQ // TM)
    seq_start = jnp.logical_and(tile_in_seq == 0, j == 0)
    zero = jnp.zeros((SUBLANES, LANES), F32)
    n_rt = CH // SUBLANES

    for L in range(N_LANE_TILES):
        for r in range(0, CONV_STG_H, SUBLANES):
            uext2[L, _rows2(r), :] = jnp.where(seq_start, zero, uext2[L, _rows2(CH + r), :])
        for r in range(0, POOL_STG_H, SUBLANES):
            zext2[L, _rows2(r), :] = jnp.where(seq_start, zero, zext2[L, _rows2(CH + r), :])
    for L in range(N_LANE_TILES):
        for rt in range(n_rt):
            r0 = rt * SUBLANES
            uext2[L, _rows2(r0 + CONV_STG_H), :] = u_ref[r0:r0 + SUBLANES, _lane(L)]
            zext2[L, _rows2(r0 + POOL_STG_H), :] = z_ref[r0:r0 + SUBLANES, _lane(L)]

    conv_done = []
    for L in range(N_LANE_TILES):
        bias = jnp.broadcast_to(cb_ref[0:1, _lane(L)], (SUBLANES, LANES))
        accs = [bias] * n_rt
        for k in range(CONV_W):
            wk = cw_ref[k:k + 1, _lane(L)]
            for rt in range(n_rt):
                r = rt * SUBLANES + k + (CONV_STG_H - CONV_H)
                accs[rt] = accs[rt] + wk * uext2[L, _rows2(r), :]
        for rt in range(n_rt):
            conv_sc[rt * SUBLANES:(rt + 1) * SUBLANES, _lane(L)] = accs[rt]
        conv_done.append(accs)

    pos0 = tile_in_seq * TM + j * CH
    row = lax.broadcasted_iota(jnp.int32, (SUBLANES, LANES), 0)
    inv_cnt = {}
    for win in POOL_WINDOWS:
        for rt in range(n_rt):
            cnt = jnp.minimum(win, pos0 + rt * SUBLANES + row + 1).astype(F32)
            inv_cnt[win, rt] = 1.0 / cnt
    pool_done = []
    for L in range(N_LANE_TILES):
        win = POOL_WINDOWS[L // TILES_PER_GROUP]
        ds = []
        for rt in range(n_rt):
            base = rt * SUBLANES + POOL_STG_H
            tok = zext2[L, _rows2(base), :]
            s = tok
            for m in range(1, win):
                s = s + zext2[L, _rows2(base - m), :]
            d = s * inv_cnt[win, rt] - tok
            d_sc[rt * SUBLANES:(rt + 1) * SUBLANES, _lane(L)] = d
            ds.append(d)
        pool_done.append(ds)

    rows = pl.ds(pl.multiple_of(j * CH, CH), CH)
    c = _ln_swish(conv_sc[...], lng_ref[...], lnb_ref[...])
    mix_sc[rows, 0:D_CONV] = c.astype(BF16)
    mix_sc[rows, D_CONV:D_MODEL] = d_sc[...].astype(BF16)

    sc_ref[...] = u_ref[CH - CONV_H:CH, :]
    sp_ref[...] = z_ref[CH - POOL_H:CH, :]
    ln_done = [c[r:r + SUBLANES, _lane(L)] for r in range(0, CH, SUBLANES) for L in range(N_LANE_TILES)]
    return conv_done, pool_done, ln_done


def _tied(x, deps, zero_ref):
    if not deps:
        return x
    bits = [lax.bitcast_convert_type(d, jnp.int32) for d in deps]
    while len(bits) > 1:
        bits = [a | b for a, b in zip(bits[0::2], bits[1::2])] + bits[len(bits) & ~1:]
    z = lax.bitcast_convert_type(bits[0] & zero_ref[...], F32)[0:1, :].astype(x.dtype)
    return x + jnp.concatenate([z] * (x.shape[1] // LANES), axis=1)


def _prompt_mix_dot(mix_sc, pw_ref, ps_ref, wo_ref):
    for g in range(N_POOL_GROUPS):
        cols = slice(D_CONV + g * POOL_GROUP_W, D_CONV + (g + 1) * POOL_GROUP_W)
        mix_sc[:, cols] = _group_map(mix_sc[:, cols], g, pw_ref, ps_ref).astype(BF16)
    return jnp.dot(mix_sc[...], wo_ref[...], preferred_element_type=F32)


def _ffn_first(j, x_ref, mix_dot, g2_ref, o_ref, h2_ref):
    @pl.when(j == 0)
    def _():
        x1 = x_ref[...] + mix_dot()
        o_ref[...] = x1
        h2_ref[...] = _rms(x1, g2_ref[...]).astype(BF16)


def _ffn_tile(w1_ref, w2_ref, o_ref, h2_ref):
    f = jnp.dot(h2_ref[...], w1_ref[...], preferred_element_type=F32)
    f = jnp.square(jnp.maximum(f, 0.0)).astype(BF16)
    o_ref[...] += jnp.dot(f, w2_ref[...], preferred_element_type=F32)


def _ffn_tile_tied(w1_ref, w2_ref, o_ref, h2_ref, f_sc, after):
    n1 = TF // MXU_COLS
    for n in range(n1):
        cols = slice(n * MXU_COLS, (n + 1) * MXU_COLS)
        r = jnp.dot(h2_ref[...], after(n, w1_ref[:, cols]), preferred_element_type=F32)
        f_sc[:, cols] = jnp.square(jnp.maximum(r, 0.0)).astype(BF16)
    for n in range(D_MODEL // MXU_COLS):
        cols = slice(n * MXU_COLS, (n + 1) * MXU_COLS)
        o_ref[:, cols] += jnp.dot(f_sc[...], after(n1 + n, w2_ref[:, cols]), preferred_element_type=F32)


def _final_norm(j, o_ref, gf_ref):
    @pl.when(j == N_F - 1)
    def _():
        o_ref[...] = _rms(o_ref[...], gf_ref[...])


def _mix_ffn_kernel(n_jobs, final_norm, n_tiles,
                    u_ref, z_ref, cw_ref, cb_ref, lng_ref, lnb_ref, pw_ref, ps_ref,
                    x_ref, wo_ref, g2_ref, w1_ref, w2_ref, gf_ref, zero_ref, *rest):
    cast_src, rest = rest[:n_jobs], rest[n_jobs:]
    o_ref, sc_ref, sp_ref = rest[:3]
    cast_dst, rest = rest[3:3 + n_jobs], rest[3 + n_jobs:]
    h2_ref, mix_sc, f_sc, uext2, zext2, conv_sc, d_sc = rest
    i = pl.program_id(0)
    j = pl.program_id(1)
    tile = jnp.minimum(i, n_tiles - 1)

    _run_casts(cast_src, cast_dst)

    def mixer():
        return _mixer_chunk(tile, j, u_ref, z_ref, cw_ref, cb_ref, lng_ref, lnb_ref,
                            mix_sc, sc_ref, sp_ref, uext2, zext2, conv_sc, d_sc)

    @pl.when(i == 0)
    def _():
        @pl.when(j == 0)
        def _():
            zero = jnp.zeros((SUBLANES, LANES), F32)
            for L in range(N_LANE_TILES):
                for r in range(0, CONV_STG_H, SUBLANES):
                    uext2[L, _rows2(CH + r), :] = zero
                for r in range(0, POOL_STG_H, SUBLANES):
                    zext2[L, _rows2(CH + r), :] = zero
        mixer()

    @pl.when(i > 0)
    def _():
        mix_dot = functools.partial(_prompt_mix_dot, mix_sc, pw_ref, ps_ref, wo_ref)
        _ffn_first(j, x_ref, mix_dot, g2_ref, o_ref, h2_ref)
        conv_done, pool_done, ln_done = mixer()

        pieces = {0: conv_done[0:3], 1: conv_done[3:6], 2: conv_done[6:8], 3: pool_done, 4: [ln_done]}

        def after(b, w):
            deps = [v for piece in pieces.get(b // 2, ()) for v in piece]
            if not deps:
                return w
            k_last = w.shape[0] - MXU_COLS
            return jnp.concatenate([w[:k_last], _tied(w[k_last:], deps, zero_ref)], axis=0)

        _ffn_tile_tied(w1_ref, w2_ref, o_ref, h2_ref, f_sc, after)
        if final_norm:
            _final_norm(j, o_ref, gf_ref)


def _mix_ffn(layer, u, z, cw, cb, lng, lnb, pw, ps, x, wo, g2, w1, w2, gf, final_norm, cast_jobs=()):
    rows = x.shape[0]
    n_tiles = rows // TM
    tiles_per_seq = SEQ // TM
    mix_tile = lambda i: jnp.minimum(i, n_tiles - 1)
    ffn_tile = lambda i: jnp.maximum(i - 1, 0)
    ffn_j = lambda i, j: jnp.where(i > 0, j, 0)
    const2 = lambda i, j: (0, 0)
    c_in, c_out, c_shape = _cast_specs(
        cast_jobs, n_tiles * N_F, lambda i, j: ffn_tile(i) * N_F + ffn_j(i, j))
    return pl.pallas_call(
        functools.partial(_mix_ffn_kernel, len(cast_jobs), final_norm, n_tiles),
        grid=(n_tiles + 1, N_F),
        in_specs=[
            pl.BlockSpec((CH, D_CONV), lambda i, j: (mix_tile(i) * N_F + j, 0)),
            pl.BlockSpec((CH, D_POOL), lambda i, j: (mix_tile(i) * N_F + j, 0)),
        ] + [_layer_spec(a, layer) for a in (cw, cb, lng, lnb, pw, ps)] + [
            pl.BlockSpec((TM, D_MODEL), lambda i, j: (ffn_tile(i), 0)),
            pl.BlockSpec((D_MODEL, D_MODEL), const2, pipeline_mode=pl.Buffered(1)),
            _layer_spec(g2, layer),
            pl.BlockSpec((D_MODEL, TF), lambda i, j: (0, ffn_j(i, j))),
            pl.BlockSpec((TF, D_MODEL), lambda i, j: (ffn_j(i, j), 0)),
            pl.BlockSpec((1, D_MODEL), const2),
            pl.BlockSpec((SUBLANES, LANES), const2),
        ] + c_in,
        out_specs=[
            pl.BlockSpec((TM, D_MODEL), lambda i, j: (ffn_tile(i), 0)),
            pl.BlockSpec((None, CONV_H, D_CONV), lambda i, j: (mix_tile(i) // tiles_per_seq, 0, 0)),
            pl.BlockSpec((None, POOL_H, D_POOL), lambda i, j: (mix_tile(i) // tiles_per_seq, 0, 0)),
        ] + c_out,
        out_shape=[
            jax.ShapeDtypeStruct((rows, D_MODEL), F32),
            jax.ShapeDtypeStruct((BATCH, CONV_H, D_CONV), F32),
            jax.ShapeDtypeStruct((BATCH, POOL_H, D_POOL), F32),
        ] + c_shape,
        scratch_shapes=[
            pltpu.VMEM((TM, D_MODEL), BF16),
            pltpu.VMEM((TM, D_MODEL), BF16),
            pltpu.VMEM((TM, TF), BF16),
            pltpu.VMEM((N_LANE_TILES, 2 * (CH + CONV_STG_H), LANES), F32),
            pltpu.VMEM((N_LANE_TILES, 2 * (CH + POOL_STG_H), LANES), F32),
            pltpu.VMEM((CH, D_CONV), F32),
            pltpu.VMEM((CH, D_POOL), F32),
        ],
        compiler_params=pltpu.CompilerParams(
            dimension_semantics=("arbitrary", "arbitrary"), vmem_limit_bytes=VMEM_LIMIT),
        name="mix_ffn",
    )(u, z, cw, cb, lng, lnb, pw, ps, x, wo, g2, w1, w2, gf, jnp.zeros((SUBLANES, LANES), jnp.int32),
      *[w_ for w_, _ in cast_jobs])


def _mix_sample_kernel(layer, u_ref, z_ref, hc_ref, hp_ref, cw_ref, cb_ref, lng_ref, lnb_ref, pw_ref, ps_ref,
                       *rest):
    if layer > 0:
        rest = rest[2:]
    mix_ref, nc_ref, np_ref = rest

    def uext(j):
        return hc_ref[j] if j < CONV_H else u_ref[j - CONV_H]

    def zext(j):
        return hp_ref[j] if j < POOL_H else z_ref[j - POOL_H]

    for t in range(DEC_SEQ):
        acc = jnp.broadcast_to(cb_ref[...], (SB, D_CONV))
        for k in range(CONV_W):
            acc = acc + cw_ref[k:k + 1, :] * uext(t + k)
        mix_ref[t, :, 0:D_CONV] = _ln_swish(acc, lng_ref[...], lnb_ref[...]).astype(BF16)

        for g in range(N_POOL_GROUPS):
            win = POOL_WINDOWS[g]
            cols = slice(g * POOL_GROUP_W, (g + 1) * POOL_GROUP_W)
            tok = zext(POOL_H + t)[:, cols]
            s = tok
            for m in range(1, win):
                s = s + zext(POOL_H + t - m)[:, cols]
            cnt = float(min(win, PAST_LEN + t + 1))
            p = _group_map(s / cnt - tok, g, pw_ref, ps_ref)
            mix_ref[t, :, D_CONV + g * POOL_GROUP_W:D_CONV + (g + 1) * POOL_GROUP_W] = p.astype(BF16)

    for j in range(CONV_H):
        nc_ref[j] = uext(j + DEC_SEQ)
    for j in range(POOL_H):
        np_ref[j] = zext(j + DEC_SEQ)


def _mix_sample(layer, u, z, hist_c, hist_p, cw, cb, lng, lnb, pw, ps, prev_states):
    tok_map = lambda s: (0, s, 0)
    st_map = lambda s: (layer, 0, s, 0)
    n_in = 10
    return pl.pallas_call(
        functools.partial(_mix_sample_kernel, layer),
        grid=(DEC_BATCH // SB,),
        in_specs=[
            pl.BlockSpec((DEC_SEQ, SB, D_CONV), tok_map),
            pl.BlockSpec((DEC_SEQ, SB, D_POOL), tok_map),
            pl.BlockSpec((None, CONV_H, SB, D_CONV), st_map),
            pl.BlockSpec((None, POOL_H, SB, D_POOL), st_map),
        ] + [_layer_spec(a, layer) for a in (cw, cb, lng, lnb, pw, ps)]
        + [pl.BlockSpec(memory_space=pl.ANY)] * len(prev_states),
        out_specs=[
            pl.BlockSpec((DEC_SEQ, SB, D_MODEL), tok_map),
            pl.BlockSpec((None, CONV_H, SB, D_CONV), st_map),
            pl.BlockSpec((None, POOL_H, SB, D_POOL), st_map),
        ],
        out_shape=[
            jax.ShapeDtypeStruct((DEC_SEQ, DEC_BATCH, D_MODEL), BF16),
            jax.ShapeDtypeStruct((DEPTH, CONV_H, DEC_BATCH, D_CONV), F32),
            jax.ShapeDtypeStruct((DEPTH, POOL_H, DEC_BATCH, D_POOL), F32),
        ],
        input_output_aliases={n_in + k: 1 + k for k in range(len(prev_states))},
        compiler_params=pltpu.CompilerParams(
            dimension_semantics=("arbitrary",), vmem_limit_bytes=VMEM_LIMIT),
        name="mix_sample",
    )(u, z, hist_c, hist_p, cw, cb, lng, lnb, pw, ps, *prev_states)


def _out_ffn_kernel(final_norm, x_ref, mix_ref, wo_ref, g2_ref, w1_ref, w2_ref, gf_ref, o_ref, h2_ref):
    j = pl.program_id(1)
    mix_dot = lambda: jnp.dot(mix_ref[...], wo_ref[...], preferred_element_type=F32)
    _ffn_first(j, x_ref, mix_dot, g2_ref, o_ref, h2_ref)
    _ffn_tile(w1_ref, w2_ref, o_ref, h2_ref)
    if final_norm:
        _final_norm(j, o_ref, gf_ref)


def _out_ffn(layer, x, mix, wo, g2, w1, w2, gf, final_norm):
    rows = x.shape[0]
    return pl.pallas_call(
        functools.partial(_out_ffn_kernel, final_norm),
        grid=(rows // TM, N_F),
        in_specs=[
            pl.BlockSpec((TM, D_MODEL), lambda i, j: (i, 0)),
            pl.BlockSpec((TM, D_MODEL), lambda i, j: (i, 0)),
            pl.BlockSpec((D_MODEL, D_MODEL), lambda i, j: (0, 0), pipeline_mode=pl.Buffered(1)),
            _layer_spec(g2, layer),
            pl.BlockSpec((D_MODEL, TF), lambda i, j: (0, j)),
            pl.BlockSpec((TF, D_MODEL), lambda i, j: (j, 0)),
            pl.BlockSpec((1, D_MODEL), lambda i, j: (0, 0)),
        ],
        out_specs=pl.BlockSpec((TM, D_MODEL), lambda i, j: (i, 0)),
        out_shape=jax.ShapeDtypeStruct((rows, D_MODEL), F32),
        scratch_shapes=[pltpu.VMEM((TM, D_MODEL), BF16)],
        compiler_params=pltpu.CompilerParams(
            dimension_semantics=("arbitrary", "arbitrary"), vmem_limit_bytes=VMEM_LIMIT),
        name="out_ffn",
    )(x, mix, wo, g2, w1, w2, gf)


def kernel(x_prompt, x_sample, state_conv, state_pool, norm1_g, w_in, b_in, conv_w, conv_b, ln_g, ln_b, pool_w, pool_scale, w_out, norm2_g, w_ff1, w_ff2, norm_f):
    xp = x_prompt.reshape(BATCH * SEQ, D_MODEL)
    xs = x_sample.transpose(1, 0, 2).reshape(DEC_SEQ * DEC_BATCH, D_MODEL)
    hist_c = state_conv.transpose(0, 2, 1, 3)
    hist_p = state_pool.transpose(0, 2, 1, 3)
    pool_w_b = pool_w.astype(BF16)
    w_in_b, w_out_b = w_in[0].astype(BF16), w_out[0].astype(BF16)
    w_ff1_b = w_ff2_b = None

    rows3 = lambda a: a.reshape(DEPTH, 1, a.shape[-1])
    mixer_w = (conv_w, rows3(conv_b), rows3(ln_g), rows3(ln_b), pool_w_b, rows3(pool_scale))
    g1, b1, g2 = rows3(norm1_g), rows3(b_in), rows3(norm2_g)

    conv_p, pool_p = [], []
    sample_states = ()
    for l in range(DEPTH):
        final = l == DEPTH - 1
        as3 = lambda a: a.reshape(DEC_SEQ, DEC_BATCH, a.shape[-1])

        jobs = ((w_ff1, 0), (w_ff2, 0)) if l == 0 else ()
        up, zp, *cast = _in_proj(l, xp, g1, w_in_b, b1, cast_jobs=jobs)
        if l == 0:
            w_ff1_b, w_ff2_b = cast
        us, zs = _in_proj(l, xs, g1, w_in_b, b1)
        mix_s, *sample_states = _mix_sample(l, as3(us), as3(zs), hist_c, hist_p, *mixer_w, sample_states)
        ffn_w = (w_out_b, g2, w_ff1_b, w_ff2_b, norm_f[None, :])
        xs = _out_ffn(l, xs, mix_s.reshape(DEC_SEQ * DEC_BATCH, D_MODEL), *ffn_w, final)
        jobs = () if final else ((w_in, l + 1), (w_out, l + 1), (w_ff1, l + 1), (w_ff2, l + 1))
        xp, cp, pp, *cast = _mix_ffn(l, up, zp, *mixer_w, xp, *ffn_w, final, cast_jobs=jobs)
        if not final:
            w_in_b, w_out_b, w_ff1_b, w_ff2_b = cast
        conv_p.append(cp)
        pool_p.append(pp)

    conv_s, pool_s = sample_states
    y_sample = xs.reshape(DEC_SEQ, DEC_BATCH, D_MODEL).transpose(1, 0, 2)
    return (xp.reshape(BATCH, SEQ, D_MODEL), y_sample, jnp.stack(conv_p), jnp.stack(pool_p),
            conv_s.transpose(0, 2, 1, 3), pool_s.transpose(0, 2, 1, 3))
```

```python
import functools

import jax
import jax.numpy as jnp
from jax import lax
from jax.experimental import pallas as pl
from jax.experimental.pallas import tpu as pltpu

F32 = jnp.float32
BF16 = jnp.bfloat16

D_MODEL = 2048
BATCH = 4
SEQ = 2048
DEPTH = 2
DEC_BATCH = 128
DEC_SEQ = 4
PAST_LEN = 16384
D_CONV = 1024
D_POOL = 1024
N_POOL_GROUPS = 4
POOL_GROUP_W = D_POOL // N_POOL_GROUPS
POOL_WINDOWS = (2, 4, 8, 16)
POOL_MAX = 16
CONV_W = 31
D_FF = 4 * D_MODEL
D_IN = 2 * D_CONV + D_POOL
EPS = 1e-6

CONV_H = CONV_W - 1
POOL_H = POOL_MAX - 1

LANES = 128
SUBLANES = 8
BF16_ROWS = 16
MXU_COLS = 256
N_LANE_TILES = D_CONV // LANES
TILES_PER_GROUP = N_LANE_TILES // N_POOL_GROUPS
VMEM_LIMIT = 60 * 1024 * 1024

TM = 512
TF = 1024
N_F = D_FF // TF
CH = TM // N_F
CONV_STG_H = 32
POOL_STG_H = 16
SB = 32
assert SEQ % TM == 0 and CH % BF16_ROWS == 0 and CH >= CONV_STG_H


def _lane(L):
    return slice(L * LANES, (L + 1) * LANES)


def _rows2(r, n=SUBLANES):
    return pl.ds(2 * r, n, stride=2)


def _rms(x, g):
    return x * lax.rsqrt(jnp.mean(x * x, axis=-1, keepdims=True) + EPS) * g


def _ln_swish(x, g, b):
    mu = jnp.mean(x, axis=-1, keepdims=True)
    xc = x - mu
    var = jnp.mean(xc * xc, axis=-1, keepdims=True)
    y = xc * lax.rsqrt(var + EPS) * g + b
    return y * jax.nn.sigmoid(y)


def _layer_spec(a, layer):
    zeros = (0,) * (a.ndim - 1)
    return pl.BlockSpec((None,) + a.shape[1:], lambda *ids: (layer,) + zeros)


def _group_map(d, g, pw_ref, ps_ref):
    cols = slice(g * POOL_GROUP_W, (g + 1) * POOL_GROUP_W)
    return jnp.dot(d.astype(BF16), pw_ref[g], preferred_element_type=F32) * ps_ref[:, cols]


def _cast_specs(jobs, n_steps, step_of):
    in_specs, out_specs, out_shapes = [], [], []
    for w, layer in jobs:
        _, r, c = w.shape
        rows = r // n_steps
        assert rows * n_steps == r and rows % BF16_ROWS == 0
        in_specs.append(pl.BlockSpec(
            (None, rows, c), lambda *ids, layer=layer: (layer, step_of(*ids), 0)))
        out_specs.append(pl.BlockSpec((rows, c), lambda *ids: (step_of(*ids), 0)))
        out_shapes.append(jax.ShapeDtypeStruct((r, c), BF16))
    return in_specs, out_specs, out_shapes


def _run_casts(src_refs, dst_refs):
    for src, dst in zip(src_refs, dst_refs):
        dst[...] = src[...].astype(BF16)


def _in_proj_kernel(n_jobs, x_ref, g_ref, w_ref, b_ref, *rest):
    cast_src, rest = rest[:n_jobs], rest[n_jobs:]
    u_ref, z_ref = rest[:2]
    _run_casts(cast_src, rest[2:])
    h = _rms(x_ref[...], g_ref[...]).astype(BF16)

    def proj(lo, hi):
        return jnp.dot(h, w_ref[:, lo:hi], preferred_element_type=F32) + b_ref[:, lo:hi]

    a = proj(0, D_CONV)
    gate = proj(D_CONV, 2 * D_CONV)
    u_ref[...] = a * jax.nn.sigmoid(gate)
    z_ref[...] = proj(2 * D_CONV, D_IN)


def _in_proj(layer, x, g, w, b, cast_jobs=()):
    rows = x.shape[0]
    c_in, c_out, c_shape = _cast_specs(cast_jobs, rows // TM, lambda i: i)
    return pl.pallas_call(
        functools.partial(_in_proj_kernel, len(cast_jobs)),
        grid=(rows // TM,),
        in_specs=[
            pl.BlockSpec((TM, D_MODEL), lambda i: (i, 0)),
            _layer_spec(g, layer),
            pl.BlockSpec((D_MODEL, D_IN), lambda i: (0, 0), pipeline_mode=pl.Buffered(1)),
            _layer_spec(b, layer),
        ] + c_in,
        out_specs=[
            pl.BlockSpec((TM, D_CONV), lambda i: (i, 0)),
            pl.BlockSpec((TM, D_POOL), lambda i: (i, 0)),
        ] + c_out,
        out_shape=[
            jax.ShapeDtypeStruct((rows, D_CONV), F32),
            jax.ShapeDtypeStruct((rows, D_POOL), F32),
        ] + c_shape,
        compiler_params=pltpu.CompilerParams(
            dimension_semantics=("arbitrary",), vmem_limit_bytes=VMEM_LIMIT),
        name="in_proj",
    )(x, g, w, b, *[w_ for w_, _ in cast_jobs])


def _mixer_chunk(tile, j, u_ref, z_ref, cw_ref, cb_ref, lng_ref, lnb_ref,
                 mix_sc, sc_ref, sp_ref, uext2, zext2, conv_sc, d_sc):
    tile_in_seq = tile % (SEQ // TM)
    seq_start = jnp.logical_and(tile_in_seq == 0, j == 0)
    zero = jnp.zeros((SUBLANES, LANES), F32)
    n_rt = CH // SUBLANES

    for L in range(N_LANE_TILES):
        for r in range(0, CONV_STG_H, SUBLANES):
            uext2[L, _rows2(r), :] = jnp.where(seq_start, zero, uext2[L, _rows2(CH + r), :])
        for r in range(0, POOL_STG_H, SUBLANES):
            zext2[L, _rows2(r), :] = jnp.where(seq_start, zero, zext2[L, _rows2(CH + r), :])
    for L in range(N_LANE_TILES):
        for rt in range(n_rt):
            r0 = rt * SUBLANES
            uext2[L, _rows2(r0 + CONV_STG_H), :] = u_ref[r0:r0 + SUBLANES, _lane(L)]
            zext2[L, _rows2(r0 + POOL_STG_H), :] = z_ref[r0:r0 + SUBLANES, _lane(L)]

    conv_done = []
    for L in range(N_LANE_TILES):
        bias = jnp.broadcast_to(cb_ref[0:1, _lane(L)], (SUBLANES, LANES))
        accs = [bias] * n_rt
        for k in range(CONV_W):
            wk = cw_ref[k:k + 1, _lane(L)]
            for rt in range(n_rt):
                r = rt * SUBLANES + k + (CONV_STG_H - CONV_H)
                accs[rt] = accs[rt] + wk * uext2[L, _rows2(r), :]
        for rt in range(n_rt):
            conv_sc[rt * SUBLANES:(rt + 1) * SUBLANES, _lane(L)] = accs[rt]
        conv_done.append(accs)

    pos0 = tile_in_seq * TM + j * CH
    row = lax.broadcasted_iota(jnp.int32, (SUBLANES, LANES), 0)
    inv_cnt = {}
    for win in POOL_WINDOWS:
        for rt in range(n_rt):
            cnt = jnp.minimum(win, pos0 + rt * SUBLANES + row + 1).astype(F32)
            inv_cnt[win, rt] = 1.0 / cnt
    pool_done = []
    for L in range(N_LANE_TILES):
        win = POOL_WINDOWS[L // TILES_PER_GROUP]
        ds = []
        for rt in range(n_rt):
            base = rt * SUBLANES + POOL_STG_H
            tok = zext2[L, _rows2(base), :]
            s = tok
            for m in range(1, win):
                s = s + zext2[L, _rows2(base - m), :]
            d = s * inv_cnt[win, rt] - tok
            d_sc[rt * SUBLANES:(rt + 1) * SUBLANES, _lane(L)] = d
            ds.append(d)
        pool_done.append(ds)

    rows = pl.ds(pl.multiple_of(j * CH, CH), CH)
    c = _ln_swish(conv_sc[...], lng_ref[...], lnb_ref[...])
    mix_sc[rows, 0:D_CONV] = c.astype(BF16)
    mix_sc[rows, D_CONV:D_MODEL] = d_sc[...].astype(BF16)

    sc_ref[...] = u_ref[CH - CONV_H:CH, :]
    sp_ref[...] = z_ref[CH - POOL_H:CH, :]
    ln_done = [c[r:r + SUBLANES, _lane(L)] for r in range(0, CH, SUBLANES) for L in range(N_LANE_TILES)]
    return conv_done, pool_done, ln_done


def _tied(x, deps, zero_ref):
    if not deps:
        return x
    bits = [lax.bitcast_convert_type(d, jnp.int32) for d in deps]
    while len(bits) > 1:
        bits = [a | b for a, b in zip(bits[0::2], bits[1::2])] + bits[len(bits) & ~1:]
    z = lax.bitcast_convert_type(bits[0] & zero_ref[...], F32)[0:1, :].astype(x.dtype)
    return x + jnp.concatenate([z] * (x.shape[1] // LANES), axis=1)


def _prompt_mix_dot(mix_sc, pw_ref, ps_ref, wo_ref):
    for g in range(N_POOL_GROUPS):
        cols = slice(D_CONV + g * POOL_GROUP_W, D_CONV + (g + 1) * POOL_GROUP_W)
        mix_sc[:, cols] = _group_map(mix_sc[:, cols], g, pw_ref, ps_ref).astype(BF16)
    return jnp.dot(mix_sc[...], wo_ref[...], preferred_element_type=F32)


def _ffn_first(j, x_ref, mix_dot, g2_ref, o_ref, h2_ref):
    @pl.when(j == 0)
    def _():
        x1 = x_ref[...] + mix_dot()
        o_ref[...] = x1
        h2_ref[...] = _rms(x1, g2_ref[...]).astype(BF16)


def _ffn_tile(w1_ref, w2_ref, o_ref, h2_ref):
    f = jnp.dot(h2_ref[...], w1_ref[...], preferred_element_type=F32)
    f = jnp.square(jnp.maximum(f, 0.0)).astype(BF16)
    o_ref[...] += jnp.dot(f, w2_ref[...], preferred_element_type=F32)


def _ffn_tile_tied(w1_ref, w2_ref, o_ref, h2_ref, f_sc, after):
    n1 = TF // MXU_COLS
    for n in range(n1):
        cols = slice(n * MXU_COLS, (n + 1) * MXU_COLS)
        r = jnp.dot(h2_ref[...], after(n, w1_ref[:, cols]), preferred_element_type=F32)
        f_sc[:, cols] = jnp.square(jnp.maximum(r, 0.0)).astype(BF16)
    for n in range(D_MODEL // MXU_COLS):
        cols = slice(n * MXU_COLS, (n + 1) * MXU_COLS)
        o_ref[:, cols] += jnp.dot(f_sc[...], after(n1 + n, w2_ref[:, cols]), preferred_element_type=F32)


def _final_norm(j, o_ref, gf_ref):
    @pl.when(j == N_F - 1)
    def _():
        o_ref[...] = _rms(o_ref[...], gf_ref[...])


def _mix_ffn_kernel(n_jobs, final_norm, n_tiles,
                    u_ref, z_ref, cw_ref, cb_ref, lng_ref, lnb_ref, pw_ref, ps_ref,
                    x_ref, wo_ref, g2_ref, w1_ref, w2_ref, gf_ref, zero_ref, *rest):
    cast_src, rest = rest[:n_jobs], rest[n_jobs:]
    o_ref, sc_ref, sp_ref = rest[:3]
    cast_dst, rest = rest[3:3 + n_jobs], rest[3 + n_jobs:]
    h2_ref, mix_sc, f_sc, uext2, zext2, conv_sc, d_sc = rest
    i = pl.program_id(0)
    j = pl.program_id(1)
    tile = jnp.minimum(i, n_tiles - 1)

    _run_casts(cast_src, cast_dst)

    def mixer():
        return _mixer_chunk(tile, j, u_ref, z_ref, cw_ref, cb_ref, lng_ref, lnb_ref,
                            mix_sc, sc_ref, sp_ref, uext2, zext2, conv_sc, d_sc)

    @pl.when(i == 0)
    def _():
        @pl.when(j == 0)
        def _():
            zero = jnp.zeros((SUBLANES, LANES), F32)
            for L in range(N_LANE_TILES):
                for r in range(0, CONV_STG_H, SUBLANES):
                    uext2[L, _rows2(CH + r), :] = zero
                for r in range(0, POOL_STG_H, SUBLANES):
                    zext2[L, _rows2(CH + r), :] = zero
        mixer()

    @pl.when(i > 0)
    def _():
        mix_dot = functools.partial(_prompt_mix_dot, mix_sc, pw_ref, ps_ref, wo_ref)
        _ffn_first(j, x_ref, mix_dot, g2_ref, o_ref, h2_ref)
        conv_done, pool_done, ln_done = mixer()

        pieces = {1: conv_done[0:3], 2: conv_done[3:6], 3: conv_done[6:8], 4: pool_done, 5: [ln_done]}

        def after(b, w):
            return _tied(w, [v for piece in pieces.get(b // 2, ()) for v in piece], zero_ref)

        _ffn_tile_tied(w1_ref, w2_ref, o_ref, h2_ref, f_sc, after)
        if final_norm:
            _final_norm(j, o_ref, gf_ref)


def _mix_ffn(layer, u, z, cw, cb, lng, lnb, pw, ps, x, wo, g2, w1, w2, gf, final_norm, cast_jobs=()):
    rows = x.shape[0]
    n_tiles = rows // TM
    tiles_per_seq = SEQ // TM
    mix_tile = lambda i: jnp.minimum(i, n_tiles - 1)
    ffn_tile = lambda i: jnp.maximum(i - 1, 0)
    ffn_j = lambda i, j: jnp.where(i > 0, j, 0)
    const2 = lambda i, j: (0, 0)
    c_in, c_out, c_shape = _cast_specs(
        cast_jobs, n_tiles * N_F, lambda i, j: ffn_tile(i) * N_F + ffn_j(i, j))
    return pl.pallas_call(
        functools.partial(_mix_ffn_kernel, len(cast_jobs), final_norm, n_tiles),
        grid=(n_tiles + 1, N_F),
        in_specs=[
            pl.BlockSpec((CH, D_CONV), lambda i, j: (mix_tile(i) * N_F + j, 0)),
            pl.BlockSpec((CH, D_POOL), lambda i, j: (mix_tile(i) * N_F + j, 0)),
        ] + [_layer_spec(a, layer) for a in (cw, cb, lng, lnb, pw, ps)] + [
            pl.BlockSpec((TM, D_MODEL), lambda i, j: (ffn_tile(i), 0)),
            pl.BlockSpec((D_MODEL, D_MODEL), const2, pipeline_mode=pl.Buffered(1)),
            _layer_spec(g2, layer),
            pl.BlockSpec((D_MODEL, TF), lambda i, j: (0, ffn_j(i, j))),
            pl.BlockSpec((TF, D_MODEL), lambda i, j: (ffn_j(i, j), 0)),
            pl.BlockSpec((1, D_MODEL), const2),
            pl.BlockSpec((SUBLANES, LANES), const2),
        ] + c_in,
        out_specs=[
            pl.BlockSpec((TM, D_MODEL), lambda i, j: (ffn_tile(i), 0)),
            pl.BlockSpec((None, CONV_H, D_CONV), lambda i, j: (mix_tile(i) // tiles_per_seq, 0, 0)),
            pl.BlockSpec((None, POOL_H, D_POOL), lambda i, j: (mix_tile(i) // tiles_per_seq, 0, 0)),
        ] + c_out,
        out_shape=[
            jax.ShapeDtypeStruct((rows, D_MODEL), F32),
            jax.ShapeDtypeStruct((BATCH, CONV_H, D_CONV), F32),
            jax.ShapeDtypeStruct((BATCH, POOL_H, D_POOL), F32),
        ] + c_shape,
        scratch_shapes=[
            pltpu.VMEM((TM, D_MODEL), BF16),
            pltpu.VMEM((TM, D_MODEL), BF16),
            pltpu.VMEM((TM, TF), BF16),
            pltpu.VMEM((N_LANE_TILES, 2 * (CH + CONV_STG_H), LANES), F32),
            pltpu.VMEM((N_LANE_TILES, 2 * (CH + POOL_STG_H), LANES), F32),
            pltpu.VMEM((CH, D_CONV), F32),
            pltpu.VMEM((CH, D_POOL), F32),
        ],
        compiler_params=pltpu.CompilerParams(
            dimension_semantics=("arbitrary", "arbitrary"), vmem_limit_bytes=VMEM_LIMIT),
        name="mix_ffn",
    )(u, z, cw, cb, lng, lnb, pw, ps, x, wo, g2, w1, w2, gf, jnp.zeros((SUBLANES, LANES), jnp.int32),
      *[w_ for w_, _ in cast_jobs])


def _mix_sample_kernel(layer, u_ref, z_ref, hc_ref, hp_ref, cw_ref, cb_ref, lng_ref, lnb_ref, pw_ref, ps_ref,
                       *rest):
    if layer > 0:
        rest = rest[2:]
    mix_ref, nc_ref, np_ref = rest

    def uext(j):
        return hc_ref[j] if j < CONV_H else u_ref[j - CONV_H]

    def zext(j):
        return hp_ref[j] if j < POOL_H else z_ref[j - POOL_H]

    for t in range(DEC_SEQ):
        acc = jnp.broadcast_to(cb_ref[...], (SB, D_CONV))
        for k in range(CONV_W):
            acc = acc + cw_ref[k:k + 1, :] * uext(t + k)
        mix_ref[t, :, 0:D_CONV] = _ln_swish(acc, lng_ref[...], lnb_ref[...]).astype(BF16)

        for g in range(N_POOL_GROUPS):
            win = POOL_WINDOWS[g]
            cols = slice(g * POOL_GROUP_W, (g + 1) * POOL_GROUP_W)
            tok = zext(POOL_H + t)[:, cols]
            s = tok
            for m in range(1, win):
                s = s + zext(POOL_H + t - m)[:, cols]
            cnt = float(min(win, PAST_LEN + t + 1))
            p = _group_map(s / cnt - tok, g, pw_ref, ps_ref)
            mix_ref[t, :, D_CONV + g * POOL_GROUP_W:D_CONV + (g + 1) * POOL_GROUP_W] = p.astype(BF16)

    for j in range(CONV_H):
        nc_ref[j] = uext(j + DEC_SEQ)
    for j in range(POOL_H):
        np_ref[j] = zext(j + DEC_SEQ)


def _mix_sample(layer, u, z, hist_c, hist_p, cw, cb, lng, lnb, pw, ps, prev_states):
    tok_map = lambda s: (0, s, 0)
    st_map = lambda s: (layer, 0, s, 0)
    n_in = 10
    return pl.pallas_call(
        functools.partial(_mix_sample_kernel, layer),
        grid=(DEC_BATCH // SB,),
        in_specs=[
            pl.BlockSpec((DEC_SEQ, SB, D_CONV), tok_map),
            pl.BlockSpec((DEC_SEQ, SB, D_POOL), tok_map),
            pl.BlockSpec((None, CONV_H, SB, D_CONV), st_map),
            pl.BlockSpec((None, POOL_H, SB, D_POOL), st_map),
        ] + [_layer_spec(a, layer) for a in (cw, cb, lng, lnb, pw, ps)]
        + [pl.BlockSpec(memory_space=pl.ANY)] * len(prev_states),
        out_specs=[
            pl.BlockSpec((DEC_SEQ, SB, D_MODEL), tok_map),
            pl.BlockSpec((None, CONV_H, SB, D_CONV), st_map),
            pl.BlockSpec((None, POOL_H, SB, D_POOL), st_map),
        ],
        out_shape=[
            jax.ShapeDtypeStruct((DEC_SEQ, DEC_BATCH, D_MODEL), BF16),
            jax.ShapeDtypeStruct((DEPTH, CONV_H, DEC_BATCH, D_CONV), F32),
            jax.ShapeDtypeStruct((DEPTH, POOL_H, DEC_BATCH, D_POOL), F32),
        ],
        input_output_aliases={n_in + k: 1 + k for k in range(len(prev_states))},
        compiler_params=pltpu.CompilerParams(
            dimension_semantics=("arbitrary",), vmem_limit_bytes=VMEM_LIMIT),
        name="mix_sample",
    )(u, z, hist_c, hist_p, cw, cb, lng, lnb, pw, ps, *prev_states)


def _out_ffn_kernel(final_norm, x_ref, mix_ref, wo_ref, g2_ref, w1_ref, w2_ref, gf_ref, o_ref, h2_ref):
    j = pl.program_id(1)
    mix_dot = lambda: jnp.dot(mix_ref[...], wo_ref[...], preferred_element_type=F32)
    _ffn_first(j, x_ref, mix_dot, g2_ref, o_ref, h2_ref)
    _ffn_tile(w1_ref, w2_ref, o_ref, h2_ref)
    if final_norm:
        _final_norm(j, o_ref, gf_ref)


def _out_ffn(layer, x, mix, wo, g2, w1, w2, gf, final_norm):
    rows = x.shape[0]
    return pl.pallas_call(
        functools.partial(_out_ffn_kernel, final_norm),
        grid=(rows // TM, N_F),
        in_specs=[
            pl.BlockSpec((TM, D_MODEL), lambda i, j: (i, 0)),
            pl.BlockSpec((TM, D_MODEL), lambda i, j: (i, 0)),
            pl.BlockSpec((D_MODEL, D_MODEL), lambda i, j: (0, 0), pipeline_mode=pl.Buffered(1)),
            _layer_spec(g2, layer),
            pl.BlockSpec((D_MODEL, TF), lambda i, j: (0, j)),
            pl.BlockSpec((TF, D_MODEL), lambda i, j: (j, 0)),
            pl.BlockSpec((1, D_MODEL), lambda i, j: (0, 0)),
        ],
        out_specs=pl.BlockSpec((TM, D_MODEL), lambda i, j: (i, 0)),
        out_shape=jax.ShapeDtypeStruct((rows, D_MODEL), F32),
        scratch_shapes=[pltpu.VMEM((TM, D_MODEL), BF16)],
        compiler_params=pltpu.CompilerParams(
            dimension_semantics=("arbitrary", "arbitrary"), vmem_limit_bytes=VMEM_LIMIT),
        name="out_ffn",
    )(x, mix, wo, g2, w1, w2, gf)


def kernel(x_prompt, x_sample, state_conv, state_pool, norm1_g, w_in, b_in, conv_w, conv_b, ln_g, ln_b, pool_w, pool_scale, w_out, norm2_g, w_ff1, w_ff2, norm_f):
    xp = x_prompt.reshape(BATCH * SEQ, D_MODEL)
    xs = x_sample.transpose(1, 0, 2).reshape(DEC_SEQ * DEC_BATCH, D_MODEL)
    hist_c = state_conv.transpose(0, 2, 1, 3)
    hist_p = state_pool.transpose(0, 2, 1, 3)
    pool_w_b = pool_w.astype(BF16)
    w_in_b, w_out_b = w_in[0].astype(BF16), w_out[0].astype(BF16)
    w_ff1_b = w_ff2_b = None

    rows3 = lambda a: a.reshape(DEPTH, 1, a.shape[-1])
    mixer_w = (conv_w, rows3(conv_b), rows3(ln_g), rows3(ln_b), pool_w_b, rows3(pool_scale))
    g1, b1, g2 = rows3(norm1_g), rows3(b_in), rows3(norm2_g)

    conv_p, pool_p = [], []
    sample_states = ()
    for l in range(DEPTH):
        final = l == DEPTH - 1
        as3 = lambda a: a.reshape(DEC_SEQ, DEC_BATCH, a.shape[-1])

        jobs = ((w_ff1, 0), (w_ff2, 0)) if l == 0 else ()
        up, zp, *cast = _in_proj(l, xp, g1, w_in_b, b1, cast_jobs=jobs)
        if l == 0:
            w_ff1_b, w_ff2_b = cast
        us, zs = _in_proj(l, xs, g1, w_in_b, b1)
        mix_s, *sample_states = _mix_sample(l, as3(us), as3(zs), hist_c, hist_p, *mixer_w, sample_states)
        ffn_w = (w_out_b, g2, w_ff1_b, w_ff2_b, norm_f[None, :])
        xs = _out_ffn(l, xs, mix_s.reshape(DEC_SEQ * DEC_BATCH, D_MODEL), *ffn_w, final)
        jobs = () if final else ((w_in, l + 1), (w_out, l + 1), (w_ff1, l + 1), (w_ff2, l + 1))
        xp, cp, pp, *cast = _mix_ffn(l, up, zp, *mixer_w, xp, *ffn_w, final, cast_jobs=jobs)
        if not final:
            w_in_b, w_out_b, w_ff1_b, w_ff2_b = cast
        conv_p.append(cp)
        pool_p.append(pp)

    conv_s, pool_s = sample_states
    y_sample = xs.reshape(DEC_SEQ, DEC_BATCH, D_MODEL).transpose(1, 0, 2)
    return (xp.reshape(BATCH, SEQ, D_MODEL), y_sample, jnp.stack(conv_p), jnp.stack(pool_p),
            conv_s.transpose(0, 2, 1, 3), pool_s.transpose(0, 2, 1, 3))
```

```python
import functools

import jax
import jax.numpy as jnp
from jax import lax
from jax.experimental import pallas as pl
from jax.experimental.pallas import tpu as pltpu

F32 = jnp.float32
BF16 = jnp.bfloat16

D_MODEL = 2048
BATCH = 4
SEQ = 2048
DEPTH = 2
DEC_BATCH = 128
DEC_SEQ = 4
PAST_LEN = 16384
D_CONV = 1024
D_POOL = 1024
N_POOL_GROUPS = 4
POOL_GROUP_W = D_POOL // N_POOL_GROUPS
POOL_WINDOWS = (2, 4, 8, 16)
POOL_MAX = 16
CONV_W = 31
D_FF = 4 * D_MODEL
D_IN = 2 * D_CONV + D_POOL
EPS = 1e-6

CONV_H = CONV_W - 1
POOL_H = POOL_MAX - 1

LANES = 128
SUBLANES = 8
BF16_ROWS = 16
MXU_COLS = 256
N_LANE_TILES = D_CONV // LANES
TILES_PER_GROUP = N_LANE_TILES // N_POOL_GROUPS
VMEM_LIMIT = 60 * 1024 * 1024

TM = 512
TF = 1024
N_F = D_FF // TF
CH = TM // N_F
CONV_STG_H = 32
POOL_STG_H = 16
SB = 32
assert SEQ % TM == 0 and CH % BF16_ROWS == 0 and CH >= CONV_STG_H


def _lane(L):
    return slice(L * LANES, (L + 1) * LANES)


def _rows2(r, n=SUBLANES):
    return pl.ds(2 * r, n, stride=2)


def _rms(x, g):
    return x * lax.rsqrt(jnp.mean(x * x, axis=-1, keepdims=True) + EPS) * g


def _ln_swish(x, g, b):
    mu = jnp.mean(x, axis=-1, keepdims=True)
    xc = x - mu
    var = jnp.mean(xc * xc, axis=-1, keepdims=True)
    y = xc * lax.rsqrt(var + EPS) * g + b
    return y * jax.nn.sigmoid(y)


def _layer_spec(a, layer):
    zeros = (0,) * (a.ndim - 1)
    return pl.BlockSpec((None,) + a.shape[1:], lambda *ids: (layer,) + zeros)


def _group_map(d, g, pw_ref, ps_ref):
    cols = slice(g * POOL_GROUP_W, (g + 1) * POOL_GROUP_W)
    return jnp.dot(d.astype(BF16), pw_ref[g], preferred_element_type=F32) * ps_ref[:, cols]


def _cast_specs(jobs, n_steps, step_of):
    in_specs, out_specs, out_shapes = [], [], []
    for w, layer in jobs:
        _, r, c = w.shape
        rows = r // n_steps
        assert rows * n_steps == r and rows % BF16_ROWS == 0
        in_specs.append(pl.BlockSpec(
            (None, rows, c), lambda *ids, layer=layer: (layer, step_of(*ids), 0)))
        out_specs.append(pl.BlockSpec((rows, c), lambda *ids: (step_of(*ids), 0)))
        out_shapes.append(jax.ShapeDtypeStruct((r, c), BF16))
    return in_specs, out_specs, out_shapes


def _run_casts(src_refs, dst_refs):
    for src, dst in zip(src_refs, dst_refs):
        dst[...] = src[...].astype(BF16)


def _in_proj_kernel(n_jobs, x_ref, g_ref, w_ref, b_ref, *rest):
    cast_src, rest = rest[:n_jobs], rest[n_jobs:]
    u_ref, z_ref = rest[:2]
    _run_casts(cast_src, rest[2:])
    h = _rms(x_ref[...], g_ref[...]).astype(BF16)

    def proj(lo, hi):
        return jnp.dot(h, w_ref[:, lo:hi], preferred_element_type=F32) + b_ref[:, lo:hi]

    a = proj(0, D_CONV)
    gate = proj(D_CONV, 2 * D_CONV)
    u_ref[...] = a * jax.nn.sigmoid(gate)
    z_ref[...] = proj(2 * D_CONV, D_IN)


def _in_proj(layer, x, g, w, b, cast_jobs=()):
    rows = x.shape[0]
    c_in, c_out, c_shape = _cast_specs(cast_jobs, rows // TM, lambda i: i)
    return pl.pallas_call(
        functools.partial(_in_proj_kernel, len(cast_jobs)),
        grid=(rows // TM,),
        in_specs=[
            pl.BlockSpec((TM, D_MODEL), lambda i: (i, 0)),
            _layer_spec(g, layer),
            pl.BlockSpec((D_MODEL, D_IN), lambda i: (0, 0), pipeline_mode=pl.Buffered(1)),
            _layer_spec(b, layer),
        ] + c_in,
        out_specs=[
            pl.BlockSpec((TM, D_CONV), lambda i: (i, 0)),
            pl.BlockSpec((TM, D_POOL), lambda i: (i, 0)),
        ] + c_out,
        out_shape=[
            jax.ShapeDtypeStruct((rows, D_CONV), F32),
            jax.ShapeDtypeStruct((rows, D_POOL), F32),
        ] + c_shape,
        compiler_params=pltpu.CompilerParams(
            dimension_semantics=("arbitrary",), vmem_limit_bytes=VMEM_LIMIT),
        name="in_proj",
    )(x, g, w, b, *[w_ for w_, _ in cast_jobs])


def _mixer_chunk(tile, j, u_ref, z_ref, cw_ref, cb_ref, lng_ref, lnb_ref,
                 mix_sc, sc_ref, sp_ref, uext2, zext2, conv_sc, d_sc):
    tile_in_seq = tile % (SEQ // TM)
    seq_start = jnp.logical_and(tile_in_seq == 0, j == 0)
    zero = jnp.zeros((SUBLANES, LANES), F32)
    n_rt = CH // SUBLANES

    for L in range(N_LANE_TILES):
        for r in range(0, CONV_STG_H, SUBLANES):
            uext2[L, _rows2(r), :] = jnp.where(seq_start, zero, uext2[L, _rows2(CH + r), :])
        for r in range(0, POOL_STG_H, SUBLANES):
            zext2[L, _rows2(r), :] = jnp.where(seq_start, zero, zext2[L, _rows2(CH + r), :])
    for L in range(N_LANE_TILES):
        for rt in range(n_rt):
            r0 = rt * SUBLANES
            uext2[L, _rows2(r0 + CONV_STG_H), :] = u_ref[r0:r0 + SUBLANES, _lane(L)]
            zext2[L, _rows2(r0 + POOL_STG_H), :] = z_ref[r0:r0 + SUBLANES, _lane(L)]

    conv_done = []
    for L in range(N_LANE_TILES):
        bias = jnp.broadcast_to(cb_ref[0:1, _lane(L)], (SUBLANES, LANES))
        accs = [bias] * n_rt
        for k in range(CONV_W):
            wk = cw_ref[k:k + 1, _lane(L)]
            for rt in range(n_rt):
                r = rt * SUBLANES + k + (CONV_STG_H - CONV_H)
                accs[rt] = accs[rt] + wk * uext2[L, _rows2(r), :]
        for rt in range(n_rt):
            conv_sc[rt * SUBLANES:(rt + 1) * SUBLANES, _lane(L)] = accs[rt]
        conv_done.append(accs)

    pos0 = tile_in_seq * TM + j * CH
    row = lax.broadcasted_iota(jnp.int32, (SUBLANES, LANES), 0)
    inv_cnt = {}
    for win in POOL_WINDOWS:
        for rt in range(n_rt):
            cnt = jnp.minimum(win, pos0 + rt * SUBLANES + row + 1).astype(F32)
            inv_cnt[win, rt] = 1.0 / cnt
    pool_done = []
    for L in range(N_LANE_TILES):
        win = POOL_WINDOWS[L // TILES_PER_GROUP]
        ds = []
        for rt in range(n_rt):
            base = rt * SUBLANES + POOL_STG_H
            tok = zext2[L, _rows2(base), :]
            s = tok
            for m in range(1, win):
                s = s + zext2[L, _rows2(base - m), :]
            d = s * inv_cnt[win, rt] - tok
            d_sc[rt * SUBLANES:(rt + 1) * SUBLANES, _lane(L)] = d
            ds.append(d)
        pool_done.append(ds)

    rows = pl.ds(pl.multiple_of(j * CH, CH), CH)
    c = _ln_swish(conv_sc[...], lng_ref[...], lnb_ref[...])
    mix_sc[rows, 0:D_CONV] = c.astype(BF16)
    mix_sc[rows, D_CONV:D_MODEL] = d_sc[...].astype(BF16)

    sc_ref[...] = u_ref[CH - CONV_H:CH, :]
    sp_ref[...] = z_ref[CH - POOL_H:CH, :]
    ln_done = [c[r:r + SUBLANES, _lane(L)] for r in range(0, CH, SUBLANES) for L in range(N_LANE_TILES)]
    return conv_done, pool_done, ln_done


def _tied(x, deps, zero_ref):
    if not deps:
        return x
    bits = lax.bitcast_convert_type(deps[0], jnp.int32)
    for d in deps[1:]:
        bits = bits | lax.bitcast_convert_type(d, jnp.int32)
    z = lax.bitcast_convert_type(bits & zero_ref[...], F32)[0:1, :].astype(x.dtype)
    return x + jnp.concatenate([z] * (x.shape[1] // LANES), axis=1)


def _prompt_mix_dot(mix_sc, pw_ref, ps_ref, wo_ref):
    for g in range(N_POOL_GROUPS):
        cols = slice(D_CONV + g * POOL_GROUP_W, D_CONV + (g + 1) * POOL_GROUP_W)
        mix_sc[:, cols] = _group_map(mix_sc[:, cols], g, pw_ref, ps_ref).astype(BF16)
    return jnp.dot(mix_sc[...], wo_ref[...], preferred_element_type=F32)


def _ffn_first(j, x_ref, mix_dot, g2_ref, o_ref, h2_ref):
    @pl.when(j == 0)
    def _():
        x1 = x_ref[...] + mix_dot()
        o_ref[...] = x1
        h2_ref[...] = _rms(x1, g2_ref[...]).astype(BF16)


def _ffn_tile(w1_ref, w2_ref, o_ref, h2_ref):
    f = jnp.dot(h2_ref[...], w1_ref[...], preferred_element_type=F32)
    f = jnp.square(jnp.maximum(f, 0.0)).astype(BF16)
    o_ref[...] += jnp.dot(f, w2_ref[...], preferred_element_type=F32)


def _ffn_tile_tied(w1_ref, w2_ref, o_ref, h2_ref, f_sc, after):
    n1 = TF // MXU_COLS
    for n in range(n1):
        cols = slice(n * MXU_COLS, (n + 1) * MXU_COLS)
        r = jnp.dot(h2_ref[...], after(n, w1_ref[:, cols]), preferred_element_type=F32)
        f_sc[:, cols] = jnp.square(jnp.maximum(r, 0.0)).astype(BF16)
    for n in range(D_MODEL // MXU_COLS):
        cols = slice(n * MXU_COLS, (n + 1) * MXU_COLS)
        o_ref[:, cols] += jnp.dot(f_sc[...], after(n1 + n, w2_ref[:, cols]), preferred_element_type=F32)


def _final_norm(j, o_ref, gf_ref):
    @pl.when(j == N_F - 1)
    def _():
        o_ref[...] = _rms(o_ref[...], gf_ref[...])


def _mix_ffn_kernel(n_jobs, final_norm, n_tiles,
                    u_ref, z_ref, cw_ref, cb_ref, lng_ref, lnb_ref, pw_ref, ps_ref,
                    x_ref, wo_ref, g2_ref, w1_ref, w2_ref, gf_ref, zero_ref, *rest):
    cast_src, rest = rest[:n_jobs], rest[n_jobs:]
    o_ref, sc_ref, sp_ref = rest[:3]
    cast_dst, rest = rest[3:3 + n_jobs], rest[3 + n_jobs:]
    h2_ref, mix_sc, f_sc, uext2, zext2, conv_sc, d_sc = rest
    i = pl.program_id(0)
    j = pl.program_id(1)
    tile = jnp.minimum(i, n_tiles - 1)

    _run_casts(cast_src, cast_dst)

    def mixer():
        return _mixer_chunk(tile, j, u_ref, z_ref, cw_ref, cb_ref, lng_ref, lnb_ref,
                            mix_sc, sc_ref, sp_ref, uext2, zext2, conv_sc, d_sc)

    @pl.when(i == 0)
    def _():
        @pl.when(j == 0)
        def _():
            zero = jnp.zeros((SUBLANES, LANES), F32)
            for L in range(N_LANE_TILES):
                for r in range(0, CONV_STG_H, SUBLANES):
                    uext2[L, _rows2(CH + r), :] = zero
                for r in range(0, POOL_STG_H, SUBLANES):
                    zext2[L, _rows2(CH + r), :] = zero
        mixer()

    @pl.when(i > 0)
    def _():
        mix_dot = functools.partial(_prompt_mix_dot, mix_sc, pw_ref, ps_ref, wo_ref)
        _ffn_first(j, x_ref, mix_dot, g2_ref, o_ref, h2_ref)
        conv_done, pool_done, ln_done = mixer()

        pieces = {1: conv_done[0:3], 2: conv_done[3:6], 3: conv_done[6:8], 4: pool_done, 5: [ln_done]}

        def after(b, w):
            return _tied(w, [v for piece in pieces.get(b // 2, ()) for v in piece], zero_ref)

        _ffn_tile_tied(w1_ref, w2_ref, o_ref, h2_ref, f_sc, after)
        if final_norm:
            _final_norm(j, o_ref, gf_ref)


def _mix_ffn(layer, u, z, cw, cb, lng, lnb, pw, ps, x, wo, g2, w1, w2, gf, final_norm, cast_jobs=()):
    rows = x.shape[0]
    n_tiles = rows // TM
    tiles_per_seq = SEQ // TM
    mix_tile = lambda i: jnp.minimum(i, n_tiles - 1)
    ffn_tile = lambda i: jnp.maximum(i - 1, 0)
    ffn_j = lambda i, j: jnp.where(i > 0, j, 0)
    const2 = lambda i, j: (0, 0)
    c_in, c_out, c_shape = _cast_specs(
        cast_jobs, n_tiles * N_F, lambda i, j: ffn_tile(i) * N_F + ffn_j(i, j))
    return pl.pallas_call(
        functools.partial(_mix_ffn_kernel, len(cast_jobs), final_norm, n_tiles),
        grid=(n_tiles + 1, N_F),
        in_specs=[
            pl.BlockSpec((CH, D_CONV), lambda i, j: (mix_tile(i) * N_F + j, 0)),
            pl.BlockSpec((CH, D_POOL), lambda i, j: (mix_tile(i) * N_F + j, 0)),
        ] + [_layer_spec(a, layer) for a in (cw, cb, lng, lnb, pw, ps)] + [
            pl.BlockSpec((TM, D_MODEL), lambda i, j: (ffn_tile(i), 0)),
            pl.BlockSpec((D_MODEL, D_MODEL), const2, pipeline_mode=pl.Buffered(1)),
            _layer_spec(g2, layer),
            pl.BlockSpec((D_MODEL, TF), lambda i, j: (0, ffn_j(i, j))),
            pl.BlockSpec((TF, D_MODEL), lambda i, j: (ffn_j(i, j), 0)),
            pl.BlockSpec((1, D_MODEL), const2),
            pl.BlockSpec((SUBLANES, LANES), const2),
        ] + c_in,
        out_specs=[
            pl.BlockSpec((TM, D_MODEL), lambda i, j: (ffn_tile(i), 0)),
            pl.BlockSpec((None, CONV_H, D_CONV), lambda i, j: (mix_tile(i) // tiles_per_seq, 0, 0)),
            pl.BlockSpec((None, POOL_H, D_POOL), lambda i, j: (mix_tile(i) // tiles_per_seq, 0, 0)),
        ] + c_out,
        out_shape=[
            jax.ShapeDtypeStruct((rows, D_MODEL), F32),
            jax.ShapeDtypeStruct((BATCH, CONV_H, D_CONV), F32),
            jax.ShapeDtypeStruct((BATCH, POOL_H, D_POOL), F32),
        ] + c_shape,
        scratch_shapes=[
            pltpu.VMEM((TM, D_MODEL), BF16),
            pltpu.VMEM((TM, D_MODEL), BF16),
            pltpu.VMEM((TM, TF), BF16),
            pltpu.VMEM((N_LANE_TILES, 2 * (CH + CONV_STG_H), LANES), F32),
            pltpu.VMEM((N_LANE_TILES, 2 * (CH + POOL_STG_H), LANES), F32),
            pltpu.VMEM((CH, D_CONV), F32),
            pltpu.VMEM((CH, D_POOL), F32),
        ],
        compiler_params=pltpu.CompilerParams(
            dimension_semantics=("arbitrary", "arbitrary"), vmem_limit_bytes=VMEM_LIMIT),
        name="mix_ffn",
    )(u, z, cw, cb, lng, lnb, pw, ps, x, wo, g2, w1, w2, gf, jnp.zeros((SUBLANES, LANES), jnp.int32),
      *[w_ for w_, _ in cast_jobs])


def _mix_sample_kernel(layer, u_ref, z_ref, hc_ref, hp_ref, cw_ref, cb_ref, lng_ref, lnb_ref, pw_ref, ps_ref,
                       *rest):
    if layer > 0:
        rest = rest[2:]
    mix_ref, nc_ref, np_ref = rest

    def uext(j):
        return hc_ref[j] if j < CONV_H else u_ref[j - CONV_H]

    def zext(j):
        return hp_ref[j] if j < POOL_H else z_ref[j - POOL_H]

    for t in range(DEC_SEQ):
        acc = jnp.broadcast_to(cb_ref[...], (SB, D_CONV))
        for k in range(CONV_W):
            acc = acc + cw_ref[k:k + 1, :] * uext(t + k)
        mix_ref[t, :, 0:D_CONV] = _ln_swish(acc, lng_ref[...], lnb_ref[...]).astype(BF16)

        for g in range(N_POOL_GROUPS):
            win = POOL_WINDOWS[g]
            cols = slice(g * POOL_GROUP_W, (g + 1) * POOL_GROUP_W)
            tok = zext(POOL_H + t)[:, cols]
            s = tok
            for m in range(1, win):
                s = s + zext(POOL_H + t - m)[:, cols]
            cnt = float(min(win, PAST_LEN + t + 1))
            p = _group_map(s / cnt - tok, g, pw_ref, ps_ref)
            mix_ref[t, :, D_CONV + g * POOL_GROUP_W:D_CONV + (g + 1) * POOL_GROUP_W] = p.astype(BF16)

    for j in range(CONV_H):
        nc_ref[j] = uext(j + DEC_SEQ)
    for j in range(POOL_H):
        np_ref[j] = zext(j + DEC_SEQ)


def _mix_sample(layer, u, z, hist_c, hist_p, cw, cb, lng, lnb, pw, ps, prev_states):
    tok_map = lambda s: (0, s, 0)
    st_map = lambda s: (layer, 0, s, 0)
    n_in = 10
    return pl.pallas_call(
        functools.partial(_mix_sample_kernel, layer),
        grid=(DEC_BATCH // SB,),
        in_specs=[
            pl.BlockSpec((DEC_SEQ, SB, D_CONV), tok_map),
            pl.BlockSpec((DEC_SEQ, SB, D_POOL), tok_map),
            pl.BlockSpec((None, CONV_H, SB, D_CONV), st_map),
            pl.BlockSpec((None, POOL_H, SB, D_POOL), st_map),
        ] + [_layer_spec(a, layer) for a in (cw, cb, lng, lnb, pw, ps)]
        + [pl.BlockSpec(memory_space=pl.ANY)] * len(prev_states),
        out_specs=[
            pl.BlockSpec((DEC_SEQ, SB, D_MODEL), tok_map),
            pl.BlockSpec((None, CONV_H, SB, D_CONV), st_map),
            pl.BlockSpec((None, POOL_H, SB, D_POOL), st_map),
        ],
        out_shape=[
            jax.ShapeDtypeStruct((DEC_SEQ, DEC_BATCH, D_MODEL), BF16),
            jax.ShapeDtypeStruct((DEPTH, CONV_H, DEC_BATCH, D_CONV), F32),
            jax.ShapeDtypeStruct((DEPTH, POOL_H, DEC_BATCH, D_POOL), F32),
        ],
        input_output_aliases={n_in + k: 1 + k for k in range(len(prev_states))},
        compiler_params=pltpu.CompilerParams(
            dimension_semantics=("arbitrary",), vmem_limit_bytes=VMEM_LIMIT),
        name="mix_sample",
    )(u, z, hist_c, hist_p, cw, cb, lng, lnb, pw, ps, *prev_states)


def _out_ffn_kernel(final_norm, x_ref, mix_ref, wo_ref, g2_ref, w1_ref, w2_ref, gf_ref, o_ref, h2_ref):
    j = pl.program_id(1)
    mix_dot = lambda: jnp.dot(mix_ref[...], wo_ref[...], preferred_element_type=F32)
    _ffn_first(j, x_ref, mix_dot, g2_ref, o_ref, h2_ref)
    _ffn_tile(w1_ref, w2_ref, o_ref, h2_ref)
    if final_norm:
        _final_norm(j, o_ref, gf_ref)


def _out_ffn(layer, x, mix, wo, g2, w1, w2, gf, final_norm):
    rows = x.shape[0]
    return pl.pallas_call(
        functools.partial(_out_ffn_kernel, final_norm),
        grid=(rows // TM, N_F),
        in_specs=[
            pl.BlockSpec((TM, D_MODEL), lambda i, j: (i, 0)),
            pl.BlockSpec((TM, D_MODEL), lambda i, j: (i, 0)),
            pl.BlockSpec((D_MODEL, D_MODEL), lambda i, j: (0, 0), pipeline_mode=pl.Buffered(1)),
            _layer_spec(g2, layer),
            pl.BlockSpec((D_MODEL, TF), lambda i, j: (0, j)),
            pl.BlockSpec((TF, D_MODEL), lambda i, j: (j, 0)),
            pl.BlockSpec((1, D_MODEL), lambda i, j: (0, 0)),
        ],
        out_specs=pl.BlockSpec((TM, D_MODEL), lambda i, j: (i, 0)),
        out_shape=jax.ShapeDtypeStruct((rows, D_MODEL), F32),
        scratch_shapes=[pltpu.VMEM((TM, D_MODEL), BF16)],
        compiler_params=pltpu.CompilerParams(
            dimension_semantics=("arbitrary", "arbitrary"), vmem_limit_bytes=VMEM_LIMIT),
        name="out_ffn",
    )(x, mix, wo, g2, w1, w2, gf)


def kernel(x_prompt, x_sample, state_conv, state_pool, norm1_g, w_in, b_in, conv_w, conv_b, ln_g, ln_b, pool_w, pool_scale, w_out, norm2_g, w_ff1, w_ff2, norm_f):
    xp = x_prompt.reshape(BATCH * SEQ, D_MODEL)
    xs = x_sample.transpose(1, 0, 2).reshape(DEC_SEQ * DEC_BATCH, D_MODEL)
    hist_c = state_conv.transpose(0, 2, 1, 3)
    hist_p = state_pool.transpose(0, 2, 1, 3)
    pool_w_b = pool_w.astype(BF16)
    w_in_b, w_out_b = w_in[0].astype(BF16), w_out[0].astype(BF16)
    w_ff1_b = w_ff2_b = None

    rows3 = lambda a: a.reshape(DEPTH, 1, a.shape[-1])
    mixer_w = (conv_w, rows3(conv_b), rows3(ln_g), rows3(ln_b), pool_w_b, rows3(pool_scale))
    g1, b1, g2 = rows3(norm1_g), rows3(b_in), rows3(norm2_g)

    conv_p, pool_p = [], []
    sample_states = ()
    for l in range(DEPTH):
        final = l == DEPTH - 1
        as3 = lambda a: a.reshape(DEC_SEQ, DEC_BATCH, a.shape[-1])

        jobs = ((w_ff1, 0), (w_ff2, 0)) if l == 0 else ()
        up, zp, *cast = _in_proj(l, xp, g1, w_in_b, b1, cast_jobs=jobs)
        if l == 0:
            w_ff1_b, w_ff2_b = cast
        us, zs = _in_proj(l, xs, g1, w_in_b, b1)
        mix_s, *sample_states = _mix_sample(l, as3(us), as3(zs), hist_c, hist_p, *mixer_w, sample_states)
        ffn_w = (w_out_b, g2, w_ff1_b, w_ff2_b, norm_f[None, :])
        xs = _out_ffn(l, xs, mix_s.reshape(DEC_SEQ * DEC_BATCH, D_MODEL), *ffn_w, final)
        jobs = () if final else ((w_in, l + 1), (w_out, l + 1), (w_ff1, l + 1), (w_ff2, l + 1))
        xp, cp, pp, *cast = _mix_ffn(l, up, zp, *mixer_w, xp, *ffn_w, final, cast_jobs=jobs)
        if not final:
            w_in_b, w_out_b, w_ff1_b, w_ff2_b = cast
        conv_p.append(cp)
        pool_p.append(pp)

    conv_s, pool_s = sample_states
    y_sample = xs.reshape(DEC_SEQ, DEC_BATCH, D_MODEL).transpose(1, 0, 2)
    return (xp.reshape(BATCH, SEQ, D_MODEL), y_sample, jnp.stack(conv_p), jnp.stack(pool_p),
            conv_s.transpose(0, 2, 1, 3), pool_s.transpose(0, 2, 1, 3))
```

```python
import functools

import jax
import jax.numpy as jnp
from jax import lax
from jax.experimental import pallas as pl
from jax.experimental.pallas import tpu as pltpu

F32 = jnp.float32
BF16 = jnp.bfloat16

D_MODEL = 2048
BATCH = 4
SEQ = 2048
DEPTH = 2
DEC_BATCH = 128
DEC_SEQ = 4
PAST_LEN = 16384
D_CONV = 1024
D_POOL = 1024
N_POOL_GROUPS = 4
POOL_GROUP_W = D_POOL // N_POOL_GROUPS
POOL_WINDOWS = (2, 4, 8, 16)
POOL_MAX = 16
CONV_W = 31
D_FF = 4 * D_MODEL
D_IN = 2 * D_CONV + D_POOL
EPS = 1e-6

CONV_H = CONV_W - 1
POOL_H = POOL_MAX - 1

LANES = 128
SUBLANES = 8
BF16_ROWS = 16
MXU_COLS = 256
N_LANE_TILES = D_CONV // LANES
TILES_PER_GROUP = N_LANE_TILES // N_POOL_GROUPS
VMEM_LIMIT = 60 * 1024 * 1024

TM = 512
TF = 1024
N_F = D_FF // TF
CH = TM // N_F
CONV_STG_H = 32
POOL_STG_H = 16
SB = 32
assert SEQ % TM == 0 and CH % BF16_ROWS == 0 and CH >= CONV_STG_H


def _lane(L):
    return slice(L * LANES, (L + 1) * LANES)


def _rows2(r, n=SUBLANES):
    return pl.ds(2 * r, n, stride=2)


def _rms(x, g):
    return x * lax.rsqrt(jnp.mean(x * x, axis=-1, keepdims=True) + EPS) * g


def _ln_swish(x, g, b):
    mu = jnp.mean(x, axis=-1, keepdims=True)
    xc = x - mu
    var = jnp.mean(xc * xc, axis=-1, keepdims=True)
    y = xc * lax.rsqrt(var + EPS) * g + b
    return y * jax.nn.sigmoid(y)


def _group_map(d, g, pw_ref, ps_ref):
    cols = slice(g * POOL_GROUP_W, (g + 1) * POOL_GROUP_W)
    return jnp.dot(d.astype(BF16), pw_ref[g], preferred_element_type=F32) * ps_ref[:, cols]


def _cast_specs(jobs, n_steps, step_of):
    in_specs, out_specs, out_shapes = [], [], []
    for w, layer in jobs:
        _, r, c = w.shape
        rows = r // n_steps
        assert rows * n_steps == r and rows % BF16_ROWS == 0
        in_specs.append(pl.BlockSpec(
            (None, rows, c), lambda *ids, layer=layer: (layer, step_of(*ids), 0)))
        out_specs.append(pl.BlockSpec((rows, c), lambda *ids: (step_of(*ids), 0)))
        out_shapes.append(jax.ShapeDtypeStruct((r, c), BF16))
    return in_specs, out_specs, out_shapes


def _run_casts(src_refs, dst_refs):
    for src, dst in zip(src_refs, dst_refs):
        dst[...] = src[...].astype(BF16)


def _in_proj_kernel(n_jobs, x_ref, g_ref, w_ref, b_ref, *rest):
    cast_src, rest = rest[:n_jobs], rest[n_jobs:]
    u_ref, z_ref = rest[:2]
    _run_casts(cast_src, rest[2:])
    h = _rms(x_ref[...], g_ref[...]).astype(BF16)

    def proj(lo, hi):
        return jnp.dot(h, w_ref[:, lo:hi], preferred_element_type=F32) + b_ref[:, lo:hi]

    a = proj(0, D_CONV)
    gate = proj(D_CONV, 2 * D_CONV)
    u_ref[...] = a * jax.nn.sigmoid(gate)
    z_ref[...] = proj(2 * D_CONV, D_IN)


def _in_proj(x, g, w, b, cast_jobs=()):
    rows = x.shape[0]
    c_in, c_out, c_shape = _cast_specs(cast_jobs, rows // TM, lambda i: i)
    return pl.pallas_call(
        functools.partial(_in_proj_kernel, len(cast_jobs)),
        grid=(rows // TM,),
        in_specs=[
            pl.BlockSpec((TM, D_MODEL), lambda i: (i, 0)),
            pl.BlockSpec((1, D_MODEL), lambda i: (0, 0)),
            pl.BlockSpec((D_MODEL, D_IN), lambda i: (0, 0), pipeline_mode=pl.Buffered(1)),
            pl.BlockSpec((1, D_IN), lambda i: (0, 0)),
        ] + c_in,
        out_specs=[
            pl.BlockSpec((TM, D_CONV), lambda i: (i, 0)),
            pl.BlockSpec((TM, D_POOL), lambda i: (i, 0)),
        ] + c_out,
        out_shape=[
            jax.ShapeDtypeStruct((rows, D_CONV), F32),
            jax.ShapeDtypeStruct((rows, D_POOL), F32),
        ] + c_shape,
        compiler_params=pltpu.CompilerParams(
            dimension_semantics=("arbitrary",), vmem_limit_bytes=VMEM_LIMIT),
        name="in_proj",
    )(x, g, w, b, *[w_ for w_, _ in cast_jobs])


def _mixer_chunk(tile, j, u_ref, z_ref, cw_ref, cb_ref, lng_ref, lnb_ref,
                 mix_sc, sc_ref, sp_ref, uext2, zext2, conv_sc, d_sc):
    tile_in_seq = tile % (SEQ // TM)
    seq_start = jnp.logical_and(tile_in_seq == 0, j == 0)
    zero = jnp.zeros((SUBLANES, LANES), F32)
    n_rt = CH // SUBLANES

    for L in range(N_LANE_TILES):
        for r in range(0, CONV_STG_H, SUBLANES):
            uext2[L, _rows2(r), :] = jnp.where(seq_start, zero, uext2[L, _rows2(CH + r), :])
        for r in range(0, POOL_STG_H, SUBLANES):
            zext2[L, _rows2(r), :] = jnp.where(seq_start, zero, zext2[L, _rows2(CH + r), :])
    for L in range(N_LANE_TILES):
        for rt in range(n_rt):
            r0 = rt * SUBLANES
            uext2[L, _rows2(r0 + CONV_STG_H), :] = u_ref[r0:r0 + SUBLANES, _lane(L)]
            zext2[L, _rows2(r0 + POOL_STG_H), :] = z_ref[r0:r0 + SUBLANES, _lane(L)]

    conv_done = []
    for L in range(N_LANE_TILES):
        bias = jnp.broadcast_to(cb_ref[0:1, _lane(L)], (SUBLANES, LANES))
        accs = [bias] * n_rt
        for k in range(CONV_W):
            wk = cw_ref[k:k + 1, _lane(L)]
            for rt in range(n_rt):
                r = rt * SUBLANES + k + (CONV_STG_H - CONV_H)
                accs[rt] = accs[rt] + wk * uext2[L, _rows2(r), :]
        for rt in range(n_rt):
            conv_sc[rt * SUBLANES:(rt + 1) * SUBLANES, _lane(L)] = accs[rt]
        conv_done.append(accs)

    pos0 = tile_in_seq * TM + j * CH
    row = lax.broadcasted_iota(jnp.int32, (SUBLANES, LANES), 0)
    inv_cnt = {}
    for win in POOL_WINDOWS:
        for rt in range(n_rt):
            cnt = jnp.minimum(win, pos0 + rt * SUBLANES + row + 1).astype(F32)
            inv_cnt[win, rt] = 1.0 / cnt
    pool_done = []
    for L in range(N_LANE_TILES):
        win = POOL_WINDOWS[L // TILES_PER_GROUP]
        ds = []
        for rt in range(n_rt):
            base = rt * SUBLANES + POOL_STG_H
            tok = zext2[L, _rows2(base), :]
            s = tok
            for m in range(1, win):
                s = s + zext2[L, _rows2(base - m), :]
            d = s * inv_cnt[win, rt] - tok
            d_sc[rt * SUBLANES:(rt + 1) * SUBLANES, _lane(L)] = d
            ds.append(d)
        pool_done.append(ds)

    rows = pl.ds(pl.multiple_of(j * CH, CH), CH)
    c = _ln_swish(conv_sc[...], lng_ref[...], lnb_ref[...])
    mix_sc[rows, 0:D_CONV] = c.astype(BF16)
    mix_sc[rows, D_CONV:D_MODEL] = d_sc[...].astype(BF16)

    sc_ref[...] = u_ref[CH - CONV_H:CH, :]
    sp_ref[...] = z_ref[CH - POOL_H:CH, :]
    ln_done = [c[r:r + SUBLANES, _lane(L)] for r in range(0, CH, SUBLANES) for L in range(N_LANE_TILES)]
    return conv_done, pool_done, ln_done


def _tied(x, deps, zero_ref):
    if not deps:
        return x
    bits = lax.bitcast_convert_type(deps[0], jnp.int32)
    for d in deps[1:]:
        bits = bits | lax.bitcast_convert_type(d, jnp.int32)
    z = lax.bitcast_convert_type(bits & zero_ref[...], F32)[0:1, :].astype(x.dtype)
    return x + jnp.concatenate([z] * (x.shape[1] // LANES), axis=1)


def _prompt_group_map(mix_sc, pw_ref, ps_ref):
    for g in range(N_POOL_GROUPS):
        cols = slice(D_CONV + g * POOL_GROUP_W, D_CONV + (g + 1) * POOL_GROUP_W)
        mix_sc[:, cols] = _group_map(mix_sc[:, cols], g, pw_ref, ps_ref).astype(BF16)


def _ffn_first(j, x_ref, mix_ref, wo_ref, g2_ref, o_ref, h2_ref, prep=None):
    @pl.when(j == 0)
    def _():
        if prep is not None:
            prep()
        half = TM // 2
        for r in range(0, TM, half):
            rows = slice(r, r + half)
            x1 = x_ref[rows, :] + jnp.dot(mix_ref[rows, :], wo_ref[...], preferred_element_type=F32)
            o_ref[rows, :] = x1
            h2_ref[rows, :] = _rms(x1, g2_ref[...]).astype(BF16)


def _ffn_tile(w1_ref, w2_ref, o_ref, h2_ref):
    f = jnp.dot(h2_ref[...], w1_ref[...], preferred_element_type=F32)
    f = jnp.square(jnp.maximum(f, 0.0)).astype(BF16)
    o_ref[...] += jnp.dot(f, w2_ref[...], preferred_element_type=F32)


def _ffn_tile_tied(w1_ref, w2_ref, o_ref, h2_ref, f_sc, after):
    n1 = TF // MXU_COLS
    for n in range(n1):
        cols = slice(n * MXU_COLS, (n + 1) * MXU_COLS)
        r = jnp.dot(h2_ref[...], after(n, w1_ref[:, cols]), preferred_element_type=F32)
        f_sc[:, cols] = jnp.square(jnp.maximum(r, 0.0)).astype(BF16)
    for n in range(D_MODEL // MXU_COLS):
        cols = slice(n * MXU_COLS, (n + 1) * MXU_COLS)
        o_ref[:, cols] += jnp.dot(f_sc[...], after(n1 + n, w2_ref[:, cols]), preferred_element_type=F32)


def _final_norm(j, o_ref, gf_ref):
    @pl.when(j == N_F - 1)
    def _():
        o_ref[...] = _rms(o_ref[...], gf_ref[...])


def _mix_ffn_kernel(n_jobs, final_norm, n_tiles,
                    u_ref, z_ref, cw_ref, cb_ref, lng_ref, lnb_ref, pw_ref, ps_ref,
                    x_ref, wo_ref, g2_ref, w1_ref, w2_ref, gf_ref, zero_ref, *rest):
    cast_src, rest = rest[:n_jobs], rest[n_jobs:]
    o_ref, sc_ref, sp_ref = rest[:3]
    cast_dst, rest = rest[3:3 + n_jobs], rest[3 + n_jobs:]
    h2_ref, mix_sc, f_sc, uext2, zext2, conv_sc, d_sc = rest
    i = pl.program_id(0)
    j = pl.program_id(1)
    tile = jnp.minimum(i, n_tiles - 1)

    _run_casts(cast_src, cast_dst)

    def mixer():
        return _mixer_chunk(tile, j, u_ref, z_ref, cw_ref, cb_ref, lng_ref, lnb_ref,
                            mix_sc, sc_ref, sp_ref, uext2, zext2, conv_sc, d_sc)

    @pl.when(i == 0)
    def _():
        @pl.when(j == 0)
        def _():
            zero = jnp.zeros((SUBLANES, LANES), F32)
            for L in range(N_LANE_TILES):
                for r in range(0, CONV_STG_H, SUBLANES):
                    uext2[L, _rows2(CH + r), :] = zero
                for r in range(0, POOL_STG_H, SUBLANES):
                    zext2[L, _rows2(CH + r), :] = zero
        mixer()

    @pl.when(i > 0)
    def _():
        prep = functools.partial(_prompt_group_map, mix_sc, pw_ref, ps_ref)
        _ffn_first(j, x_ref, mix_sc, wo_ref, g2_ref, o_ref, h2_ref, prep)
        conv_done, pool_done, ln_done = mixer()

        pieces = {1: conv_done[0:3], 2: conv_done[3:6], 3: conv_done[6:8], 4: pool_done, 5: [ln_done]}

        def after(b, w):
            return _tied(w, [v for piece in pieces.get(b // 2, ()) for v in piece], zero_ref)

        _ffn_tile_tied(w1_ref, w2_ref, o_ref, h2_ref, f_sc, after)
        if final_norm:
            _final_norm(j, o_ref, gf_ref)


def _mix_ffn(u, z, cw, cb, lng, lnb, pw, ps, x, wo, g2, w1, w2, gf, final_norm, cast_jobs=()):
    rows = x.shape[0]
    n_tiles = rows // TM
    tiles_per_seq = SEQ // TM
    mix_tile = lambda i: jnp.minimum(i, n_tiles - 1)
    ffn_tile = lambda i: jnp.maximum(i - 1, 0)
    ffn_j = lambda i, j: jnp.where(i > 0, j, 0)
    const2 = lambda i, j: (0, 0)
    c_in, c_out, c_shape = _cast_specs(
        cast_jobs, n_tiles * N_F, lambda i, j: ffn_tile(i) * N_F + ffn_j(i, j))
    return pl.pallas_call(
        functools.partial(_mix_ffn_kernel, len(cast_jobs), final_norm, n_tiles),
        grid=(n_tiles + 1, N_F),
        in_specs=[
            pl.BlockSpec((CH, D_CONV), lambda i, j: (mix_tile(i) * N_F + j, 0)),
            pl.BlockSpec((CH, D_POOL), lambda i, j: (mix_tile(i) * N_F + j, 0)),
            pl.BlockSpec((CONV_W, D_CONV), const2),
            pl.BlockSpec((1, D_CONV), const2),
            pl.BlockSpec((1, D_CONV), const2),
            pl.BlockSpec((1, D_CONV), const2),
            pl.BlockSpec((N_POOL_GROUPS, POOL_GROUP_W, POOL_GROUP_W), lambda i, j: (0, 0, 0)),
            pl.BlockSpec((1, D_POOL), const2),
            pl.BlockSpec((TM, D_MODEL), lambda i, j: (ffn_tile(i), 0)),
            pl.BlockSpec((D_MODEL, D_MODEL), const2, pipeline_mode=pl.Buffered(1)),
            pl.BlockSpec((1, D_MODEL), const2),
            pl.BlockSpec((D_MODEL, TF), lambda i, j: (0, ffn_j(i, j))),
            pl.BlockSpec((TF, D_MODEL), lambda i, j: (ffn_j(i, j), 0)),
            pl.BlockSpec((1, D_MODEL), const2),
            pl.BlockSpec((SUBLANES, LANES), const2),
        ] + c_in,
        out_specs=[
            pl.BlockSpec((TM, D_MODEL), lambda i, j: (ffn_tile(i), 0)),
            pl.BlockSpec((None, CONV_H, D_CONV), lambda i, j: (mix_tile(i) // tiles_per_seq, 0, 0)),
            pl.BlockSpec((None, POOL_H, D_POOL), lambda i, j: (mix_tile(i) // tiles_per_seq, 0, 0)),
        ] + c_out,
        out_shape=[
            jax.ShapeDtypeStruct((rows, D_MODEL), F32),
            jax.ShapeDtypeStruct((BATCH, CONV_H, D_CONV), F32),
            jax.ShapeDtypeStruct((BATCH, POOL_H, D_POOL), F32),
        ] + c_shape,
        scratch_shapes=[
            pltpu.VMEM((TM, D_MODEL), BF16),
            pltpu.VMEM((TM, D_MODEL), BF16),
            pltpu.VMEM((TM, TF), BF16),
            pltpu.VMEM((N_LANE_TILES, 2 * (CH + CONV_STG_H), LANES), F32),
            pltpu.VMEM((N_LANE_TILES, 2 * (CH + POOL_STG_H), LANES), F32),
            pltpu.VMEM((CH, D_CONV), F32),
            pltpu.VMEM((CH, D_POOL), F32),
        ],
        compiler_params=pltpu.CompilerParams(
            dimension_semantics=("arbitrary", "arbitrary"), vmem_limit_bytes=VMEM_LIMIT),
        name="mix_ffn",
    )(u, z, cw, cb, lng, lnb, pw, ps, x, wo, g2, w1, w2, gf, jnp.zeros((SUBLANES, LANES), jnp.int32),
      *[w_ for w_, _ in cast_jobs])


def _mix_sample_kernel(layer, u_ref, z_ref, hc_ref, hp_ref, cw_ref, cb_ref, lng_ref, lnb_ref, pw_ref, ps_ref,
                       *rest):
    if layer > 0:
        rest = rest[2:]
    mix_ref, nc_ref, np_ref = rest

    def uext(j):
        return hc_ref[j] if j < CONV_H else u_ref[j - CONV_H]

    def zext(j):
        return hp_ref[j] if j < POOL_H else z_ref[j - POOL_H]

    for t in range(DEC_SEQ):
        acc = jnp.broadcast_to(cb_ref[...], (SB, D_CONV))
        for k in range(CONV_W):
            acc = acc + cw_ref[k:k + 1, :] * uext(t + k)
        mix_ref[t, :, 0:D_CONV] = _ln_swish(acc, lng_ref[...], lnb_ref[...]).astype(BF16)

        for g in range(N_POOL_GROUPS):
            win = POOL_WINDOWS[g]
            cols = slice(g * POOL_GROUP_W, (g + 1) * POOL_GROUP_W)
            tok = zext(POOL_H + t)[:, cols]
            s = tok
            for m in range(1, win):
                s = s + zext(POOL_H + t - m)[:, cols]
            cnt = float(min(win, PAST_LEN + t + 1))
            p = _group_map(s / cnt - tok, g, pw_ref, ps_ref)
            mix_ref[t, :, D_CONV + g * POOL_GROUP_W:D_CONV + (g + 1) * POOL_GROUP_W] = p.astype(BF16)

    for j in range(CONV_H):
        nc_ref[j] = uext(j + DEC_SEQ)
    for j in range(POOL_H):
        np_ref[j] = zext(j + DEC_SEQ)


def _mix_sample(layer, u, z, hist_c, hist_p, cw, cb, lng, lnb, pw, ps, prev_states):
    tok_map = lambda s: (0, s, 0)
    st_map = lambda s: (layer, 0, s, 0)
    const2 = lambda s: (0, 0)
    n_in = 10
    return pl.pallas_call(
        functools.partial(_mix_sample_kernel, layer),
        grid=(DEC_BATCH // SB,),
        in_specs=[
            pl.BlockSpec((DEC_SEQ, SB, D_CONV), tok_map),
            pl.BlockSpec((DEC_SEQ, SB, D_POOL), tok_map),
            pl.BlockSpec((None, CONV_H, SB, D_CONV), st_map),
            pl.BlockSpec((None, POOL_H, SB, D_POOL), st_map),
            pl.BlockSpec((CONV_W, D_CONV), const2),
            pl.BlockSpec((1, D_CONV), const2),
            pl.BlockSpec((1, D_CONV), const2),
            pl.BlockSpec((1, D_CONV), const2),
            pl.BlockSpec((N_POOL_GROUPS, POOL_GROUP_W, POOL_GROUP_W), lambda s: (0, 0, 0)),
            pl.BlockSpec((1, D_POOL), const2),
        ] + [pl.BlockSpec(memory_space=pl.ANY)] * len(prev_states),
        out_specs=[
            pl.BlockSpec((DEC_SEQ, SB, D_MODEL), tok_map),
            pl.BlockSpec((None, CONV_H, SB, D_CONV), st_map),
            pl.BlockSpec((None, POOL_H, SB, D_POOL), st_map),
        ],
        out_shape=[
            jax.ShapeDtypeStruct((DEC_SEQ, DEC_BATCH, D_MODEL), BF16),
            jax.ShapeDtypeStruct((DEPTH, CONV_H, DEC_BATCH, D_CONV), F32),
            jax.ShapeDtypeStruct((DEPTH, POOL_H, DEC_BATCH, D_POOL), F32),
        ],
        input_output_aliases={n_in + k: 1 + k for k in range(len(prev_states))},
        compiler_params=pltpu.CompilerParams(
            dimension_semantics=("arbitrary",), vmem_limit_bytes=VMEM_LIMIT),
        name="mix_sample",
    )(u, z, hist_c, hist_p, cw, cb, lng, lnb, pw, ps, *prev_states)


def _out_ffn_kernel(final_norm, x_ref, mix_ref, wo_ref, g2_ref, w1_ref, w2_ref, gf_ref, o_ref, h2_ref):
    j = pl.program_id(1)
    _ffn_first(j, x_ref, mix_ref, wo_ref, g2_ref, o_ref, h2_ref)
    _ffn_tile(w1_ref, w2_ref, o_ref, h2_ref)
    if final_norm:
        _final_norm(j, o_ref, gf_ref)


def _out_ffn(x, mix, wo, g2, w1, w2, gf, final_norm):
    rows = x.shape[0]
    return pl.pallas_call(
        functools.partial(_out_ffn_kernel, final_norm),
        grid=(rows // TM, N_F),
        in_specs=[
            pl.BlockSpec((TM, D_MODEL), lambda i, j: (i, 0)),
            pl.BlockSpec((TM, D_MODEL), lambda i, j: (i, 0)),
            pl.BlockSpec((D_MODEL, D_MODEL), lambda i, j: (0, 0), pipeline_mode=pl.Buffered(1)),
            pl.BlockSpec((1, D_MODEL), lambda i, j: (0, 0)),
            pl.BlockSpec((D_MODEL, TF), lambda i, j: (0, j)),
            pl.BlockSpec((TF, D_MODEL), lambda i, j: (j, 0)),
            pl.BlockSpec((1, D_MODEL), lambda i, j: (0, 0)),
        ],
        out_specs=pl.BlockSpec((TM, D_MODEL), lambda i, j: (i, 0)),
        out_shape=jax.ShapeDtypeStruct((rows, D_MODEL), F32),
        scratch_shapes=[pltpu.VMEM((TM, D_MODEL), BF16)],
        compiler_params=pltpu.CompilerParams(
            dimension_semantics=("arbitrary", "arbitrary"), vmem_limit_bytes=VMEM_LIMIT),
        name="out_ffn",
    )(x, mix, wo, g2, w1, w2, gf)


def kernel(x_prompt, x_sample, state_conv, state_pool, norm1_g, w_in, b_in, conv_w, conv_b, ln_g, ln_b, pool_w, pool_scale, w_out, norm2_g, w_ff1, w_ff2, norm_f):
    xp = x_prompt.reshape(BATCH * SEQ, D_MODEL)
    xs = x_sample.transpose(1, 0, 2).reshape(DEC_SEQ * DEC_BATCH, D_MODEL)
    hist_c = state_conv.transpose(0, 2, 1, 3)
    hist_p = state_pool.transpose(0, 2, 1, 3)
    pool_w_b = pool_w.astype(BF16)
    w_in_b, w_out_b = w_in[0].astype(BF16), w_out[0].astype(BF16)
    w_ff1_b = w_ff2_b = None

    conv_p, pool_p = [], []
    sample_states = ()
    for l in range(DEPTH):
        row = lambda a: a[l][None, :]
        mixer_w = (conv_w[l], row(conv_b), row(ln_g), row(ln_b), pool_w_b[l], row(pool_scale))
        final = l == DEPTH - 1
        as3 = lambda a: a.reshape(DEC_SEQ, DEC_BATCH, a.shape[-1])

        jobs = ((w_ff1, 0), (w_ff2, 0)) if l == 0 else ()
        up, zp, *cast = _in_proj(xp, row(norm1_g), w_in_b, row(b_in), cast_jobs=jobs)
        if l == 0:
            w_ff1_b, w_ff2_b = cast
        us, zs = _in_proj(xs, row(norm1_g), w_in_b, row(b_in))
        mix_s, *sample_states = _mix_sample(l, as3(us), as3(zs), hist_c, hist_p, *mixer_w, sample_states)
        ffn_w = (w_out_b, row(norm2_g), w_ff1_b, w_ff2_b, norm_f[None, :])
        xs = _out_ffn(xs, mix_s.reshape(DEC_SEQ * DEC_BATCH, D_MODEL), *ffn_w, final)
        jobs = () if final else ((w_in, l + 1), (w_out, l + 1), (w_ff1, l + 1), (w_ff2, l + 1))
        xp, cp, pp, *cast = _mix_ffn(up, zp, *mixer_w, xp, *ffn_w, final, cast_jobs=jobs)
        if not final:
            w_in_b, w_out_b, w_ff1_b, w_ff2_b = cast
        conv_p.append(cp)
        pool_p.append(pp)

    conv_s, pool_s = sample_states
    y_sample = xs.reshape(DEC_SEQ, DEC_BATCH, D_MODEL).transpose(1, 0, 2)
    return (xp.reshape(BATCH, SEQ, D_MODEL), y_sample, jnp.stack(conv_p), jnp.stack(pool_p),
            conv_s.transpose(0, 2, 1, 3), pool_s.transpose(0, 2, 1, 3))
```
